```python
import math
import jax, jax.numpy as jnp
from jax import lax
import numpy as np

D_MODEL = 2048
BATCH = 16
SEQ = 2048
DEPTH = 4

N_MIXERS = 2
CONV_E = D_MODEL
CONV_WIDTH = 31
N_HEADS = 16
HEAD_DIM = D_MODEL // N_HEADS
N_KV_GROUPS = 4
GROUP_SIZE = N_HEADS // N_KV_GROUPS
CMP_LEN = 32
CMP_STRIDE = 16
SEL_LEN = 64
N_SELECT = 8
WINDOW = 512
QBLK = 128
NUM_BUCKETS = 32
MAX_DISTANCE = 128
Q_W = N_HEADS * HEAD_DIM
KV_W = N_KV_GROUPS * HEAD_DIM
GATE_W = 3 * N_HEADS
Z_W = Q_W
NSA_IN_W = Q_W + 6 * KV_W + GATE_W + Z_W
NEG = -1e30
FORCE = 1e6
EPS = 1e-6

kernel_name = "hybrid_conformer_nsa_gated_trunk"


def rmsnorm(x, g):
    xf = x.astype(jnp.float32)
    y = xf * lax.rsqrt(jnp.mean(xf * xf, axis=-1, keepdims=True) + EPS)
    return y.astype(x.dtype) * g


def layernorm(x, g, b):
    xf = x.astype(jnp.float32)
    mu = jnp.mean(xf, axis=-1, keepdims=True)
    var = jnp.mean(jnp.square(xf - mu), axis=-1, keepdims=True)
    y = (xf - mu) * lax.rsqrt(var + EPS)
    return y.astype(x.dtype) * g + b


def t5_bucket(dist):
    dist = jnp.maximum(dist, 0)
    max_exact = NUM_BUCKETS // 2
    d = jnp.maximum(dist, max_exact).astype(jnp.float32)
    large = max_exact + (jnp.log(d / max_exact) / math.log(MAX_DISTANCE / max_exact)
                         * (NUM_BUCKETS - max_exact)).astype(jnp.int32)
    large = jnp.minimum(large, NUM_BUCKETS - 1)
    return jnp.where(dist < max_exact, dist, large)


def conformer_mixer(h, w_in, dw_w, dw_b, ln_g, ln_b, w_out):
    u = h @ w_in
    a, b, z = jnp.split(u, 3, axis=-1)
    v = a * jax.nn.sigmoid(b)
    v = lax.conv_general_dilated(
        v, dw_w[:, None, :].astype(v.dtype), window_strides=(1,),
        padding=[(CONV_WIDTH - 1, 0)], dimension_numbers=('NWC', 'WIO', 'NWC'),
        feature_group_count=CONV_E) + dw_b
    v = layernorm(v, ln_g, ln_b)
    v = jax.nn.silu(v) * jax.nn.silu(z)
    return v @ w_out


def nsa_mixer(h, rel_bias, w_in, cmp_pos, ck_w1, ck_w2, cv_w1, cv_w2, w_out):
    B, S, _ = h.shape
    G, R, DK = N_KV_GROUPS, GROUP_SIZE, HEAD_DIM
    u = h @ w_in
    splits = [int(c) for c in np.cumsum([Q_W] + [KV_W] * 6 + [GATE_W])]
    q, kc, vc, ks, vs, kw, vw, gates, z = jnp.split(u, splits, axis=-1)
    q = q.reshape(B, S, G, R, DK)
    kc, vc, ks, vs, kw, vw = [t.reshape(B, S, G, DK) for t in (kc, vc, ks, vs, kw, vw)]
    gate = jax.nn.sigmoid(gates.astype(jnp.float32)).reshape(B, S, G, R, 3)
    scale = DK ** -0.5

    nc = (S - CMP_LEN) // CMP_STRIDE + 1
    tok_idx = np.arange(nc)[:, None] * CMP_STRIDE + np.arange(CMP_LEN)[None, :]
    cmp_end = jnp.asarray(tok_idx[:, -1], dtype=jnp.int32)

    def compress(t, w1, w2):
        blk = t[:, tok_idx] + cmp_pos[:, None, :]
        blk = blk.transpose(0, 1, 3, 2, 4).reshape(B, nc, G, CMP_LEN * DK)
        return jax.nn.silu(blk @ w1) @ w2

    Kc = compress(kc, ck_w1, ck_w2)
    Vc = compress(vc, cv_w1, cv_w2)

    ns = S // SEL_LEN
    n_sel = min(N_SELECT, ns)
    j0 = np.arange(nc)[:, None] * CMP_STRIDE
    s0 = np.arange(ns)[None, :] * SEL_LEN
    overlap = jnp.asarray(((j0 < s0 + SEL_LEN) & (j0 + CMP_LEN > s0)).astype(np.float32))
    blk_start = jnp.arange(ns, dtype=jnp.int32) * SEL_LEN
    Ks_blk = ks.reshape(B, ns, SEL_LEN, G, DK).transpose(0, 3, 1, 2, 4)
    Vs_blk = vs.reshape(B, ns, SEL_LEN, G, DK).transpose(0, 3, 1, 2, 4)
    bi = jnp.arange(B)[:, None, None, None]
    gi = jnp.arange(G)[None, None, :, None]
    tbl_g = rel_bias.reshape(NUM_BUCKETS, G, R).transpose(1, 0, 2)

    kw_pad = jnp.pad(kw, ((0, 0), (WINDOW, 0), (0, 0), (0, 0)))
    vw_pad = jnp.pad(vw, ((0, 0), (WINDOW, 0), (0, 0), (0, 0)))

    def head_bias(bucket):
        bb = rel_bias[bucket]
        return bb.reshape(bucket.shape + (G, R)).transpose(2, 3, 0, 1).astype(jnp.float32)

    def block_fn(i):
        t0 = i * QBLK
        qb = lax.dynamic_slice_in_dim(q, t0, QBLK, axis=1)
        gb = lax.dynamic_slice_in_dim(gate, t0, QBLK, axis=1)
        tq = t0 + jnp.arange(QBLK, dtype=jnp.int32)

        s_c = jnp.einsum('btgrd,bngd->bgrtn', qb, Kc).astype(jnp.float32) * scale
        dist_c = tq[:, None] - cmp_end[None, :]
        valid_c = dist_c >= 0
        s_c = jnp.where(valid_c, s_c + head_bias(t5_bucket(dist_c)), NEG)
        p_c = jnp.where(valid_c, jax.nn.softmax(s_c, axis=-1), 0.0)
        o_c = jnp.einsum('bgrtn,bngd->btgrd', p_c.astype(Vc.dtype), Vc)

        imp = jnp.einsum('bgrtn,ns->btgs', p_c, overlap)
        cur = tq // SEL_LEN
        sb = jnp.arange(ns, dtype=jnp.int32)[None, :]
        forced = (sb == 0) | (sb == cur[:, None]) | (sb == cur[:, None] - 1)
        causal_blk = blk_start[None, :] <= tq[:, None]
        imp = jnp.where(forced[None, :, None, :], FORCE, imp)
        imp = jnp.where(causal_blk[None, :, None, :], imp, NEG)
        _, idx = lax.top_k(imp, n_sel)

        K_sel = Ks_blk[bi, gi, idx].reshape(B, QBLK, G, n_sel * SEL_LEN, DK)
        V_sel = Vs_blk[bi, gi, idx].reshape(B, QBLK, G, n_sel * SEL_LEN, DK)
        kpos = (idx[..., None] * SEL_LEN + jnp.arange(SEL_LEN, dtype=jnp.int32)).reshape(B, QBLK, G, n_sel * SEL_LEN)
        dist_s = tq[None, :, None, None] - kpos
        bias_s = tbl_g[gi, t5_bucket(dist_s)].transpose(0, 2, 4, 1, 3).astype(jnp.float32)
        mask_s = (dist_s >= 0).transpose(0, 2, 1, 3)[:, :, None]
        s_s = jnp.einsum('btgrd,btgkd->bgrtk', qb, K_sel).astype(jnp.float32) * scale
        p_s = jax.nn.softmax(jnp.where(mask_s, s_s + bias_s, NEG), axis=-1)
        o_s = jnp.einsum('bgrtk,btgkd->btgrd', p_s.astype(V_sel.dtype), V_sel)

        kwb = lax.dynamic_slice_in_dim(kw_pad, t0, QBLK + WINDOW, axis=1)
        vwb = lax.dynamic_slice_in_dim(vw_pad, t0, QBLK + WINDOW, axis=1)
        kpos_w = t0 - WINDOW + jnp.arange(QBLK + WINDOW, dtype=jnp.int32)
        dist_w = tq[:, None] - kpos_w[None, :]
        mask_w = (dist_w >= 0) & (dist_w < WINDOW) & (kpos_w[None, :] >= 0)
        s_w = jnp.einsum('btgrd,bkgd->bgrtk', qb, kwb).astype(jnp.float32) * scale
        p_w = jax.nn.softmax(jnp.where(mask_w, s_w + head_bias(t5_bucket(dist_w)), NEG), axis=-1)
        o_w = jnp.einsum('bgrtk,bkgd->btgrd', p_w.astype(vwb.dtype), vwb)

        gb = gb.astype(o_c.dtype)
        return gb[..., 0:1] * o_c + gb[..., 1:2] * o_s + gb[..., 2:3] * o_w

    outs = lax.map(block_fn, jnp.arange(S // QBLK))
    o = outs.transpose(1, 0, 2, 3, 4, 5).reshape(B, S, Q_W)
    return (o * jax.nn.silu(z)) @ w_out


def setup_inputs(seed: int = 0) -> dict:
    key = jax.random.key(seed)
    keys = iter(jax.random.split(key, 128))

    def nrm(shape, scale):
        return jax.random.normal(next(keys), shape, jnp.float32) * scale

    inputs = {
        "x": nrm((BATCH, SEQ, D_MODEL), 1.0),
        "rel_bias": nrm((NUM_BUCKETS, N_HEADS), 0.5),
    }
    for i in range(DEPTH):
        p = f"l{i}_"
        inputs[p + "norm"] = 1.0 + nrm((D_MODEL,), 0.1)
        if i % N_MIXERS == 0:
            inputs[p + "w_in"] = nrm((D_MODEL, 3 * CONV_E), D_MODEL ** -0.5)
            inputs[p + "dw_w"] = nrm((CONV_WIDTH, CONV_E), CONV_WIDTH ** -0.5)
            inputs[p + "dw_b"] = nrm((CONV_E,), 0.02)
            inputs[p + "ln_g"] = 1.0 + nrm((CONV_E,), 0.1)
            inputs[p + "ln_b"] = nrm((CONV_E,), 0.02)
            inputs[p + "w_out"] = nrm((CONV_E, D_MODEL), CONV_E ** -0.5)
        else:
            inputs[p + "w_in"] = nrm((D_MODEL, NSA_IN_W), D_MODEL ** -0.5)
            inputs[p + "cmp_pos"] = nrm((CMP_LEN, HEAD_DIM), 0.5)
            inputs[p + "ck_w1"] = nrm((CMP_LEN * HEAD_DIM, HEAD_DIM), (CMP_LEN * HEAD_DIM) ** -0.5)
            inputs[p + "ck_w2"] = nrm((HEAD_DIM, HEAD_DIM), HEAD_DIM ** -0.5)
            inputs[p + "cv_w1"] = nrm((CMP_LEN * HEAD_DIM, HEAD_DIM), (CMP_LEN * HEAD_DIM) ** -0.5)
            inputs[p + "cv_w2"] = nrm((HEAD_DIM, HEAD_DIM), HEAD_DIM ** -0.5)
            inputs[p + "w_out"] = nrm((Q_W, D_MODEL), Q_W ** -0.5)
    inputs["final_norm"] = 1.0 + nrm((D_MODEL,), 0.1)
    return inputs


def reference(x, rel_bias,
              l0_norm, l0_w_in, l0_dw_w, l0_dw_b, l0_ln_g, l0_ln_b, l0_w_out,
              l1_norm, l1_w_in, l1_cmp_pos, l1_ck_w1, l1_ck_w2, l1_cv_w1, l1_cv_w2, l1_w_out,
              l2_norm, l2_w_in, l2_dw_w, l2_dw_b, l2_ln_g, l2_ln_b, l2_w_out,
              l3_norm, l3_w_in, l3_cmp_pos, l3_ck_w1, l3_ck_w2, l3_cv_w1, l3_cv_w2, l3_w_out,
              final_norm):
    layers = [
        (l0_norm, (l0_w_in, l0_dw_w, l0_dw_b, l0_ln_g, l0_ln_b, l0_w_out)),
        (l1_norm, (l1_w_in, l1_cmp_pos, l1_ck_w1, l1_ck_w2, l1_cv_w1, l1_cv_w2, l1_w_out)),
        (l2_norm, (l2_w_in, l2_dw_w, l2_dw_b, l2_ln_g, l2_ln_b, l2_w_out)),
        (l3_norm, (l3_w_in, l3_cmp_pos, l3_ck_w1, l3_ck_w2, l3_cv_w1, l3_cv_w2, l3_w_out)),
    ]
    for i in range(DEPTH):
        g, params = layers[i]
        h = rmsnorm(x, g)
        if i % N_MIXERS == 0:
            x = x + conformer_mixer(h, *params)
        else:
            x = x + nsa_mixer(h, rel_bias, *params)
    return rmsnorm(x, final_norm)
```

```python
import functools
import math

import jax
import jax.numpy as jnp
import numpy as np
from jax import lax
from jax.experimental import pallas as pl
from jax.experimental.pallas import tpu as pltpu

F32 = jnp.float32
BF16 = jnp.bfloat16

GROUP_SIZE = 4
CMP_STRIDE = 16
SEL_LEN = 64
N_SELECT = 8
WINDOW = 512
MAX_DISTANCE = 128
EPS = 1e-6
NEG = -1e30
FORCE = 1e6

TQ = 256
V7X_VMEM_LIMIT_BYTES = 56 * 1024 * 1024
SUBLANES = 8
BF16_ROWS = 16


def _params(*semantics):
    return pltpu.CompilerParams(dimension_semantics=semantics, vmem_limit_bytes=V7X_VMEM_LIMIT_BYTES)


def _sigmoid(x):
    return 1.0 / (1.0 + jnp.exp(-x))


def _dot(a, b):
    return jnp.dot(a, b, preferred_element_type=F32)


def _dot_nt(a, b):
    return lax.dot_general(a, b, (((1,), (1,)), ((), ())), preferred_element_type=F32)


def _rmsnorm_body(x_ref, g_ref, o_ref):
    x = x_ref[...]
    ms = jnp.mean(x * x, axis=-1, keepdims=True)
    o_ref[...] = (x * lax.rsqrt(ms + EPS) * g_ref[...]).astype(o_ref.dtype)


def _rmsnorm(x2d, gain, tm=512):
    m, d = x2d.shape
    return pl.pallas_call(
        _rmsnorm_body,
        out_shape=jax.ShapeDtypeStruct((m, d), BF16),
        grid=(m // tm,),
        in_specs=[pl.BlockSpec((tm, d), lambda i: (i, 0)), pl.BlockSpec((1, d), lambda i: (0, 0))],
        out_specs=pl.BlockSpec((tm, d), lambda i: (i, 0)),
        compiler_params=_params("parallel"),
        name="rmsnorm",
    )(x2d, gain.reshape(1, d))


def _out_proj_tail(lhs_ref, w_ref, x_ref, gn_ref, xo_ref, ho_ref):
    tm = xo_ref.shape[0]
    xo_ref[...] = x_ref[...] + _dot(lhs_ref[...], w_ref[...])

    def chunk(c, carry):
        r0 = pl.multiple_of(c * BF16_ROWS, BF16_ROWS)
        xn = xo_ref[pl.ds(r0, BF16_ROWS), :]
        ms = jnp.mean(xn * xn, axis=-1, keepdims=True)
        hn = xn * lax.rsqrt(ms + EPS) * gn_ref[...]
        if ho_ref is None:
            xo_ref[pl.ds(r0, BF16_ROWS), :] = hn
        else:
            ho_ref[pl.ds(r0, BF16_ROWS), :] = hn.astype(ho_ref.dtype)
        return carry

    lax.fori_loop(0, tm // BF16_ROWS, chunk, 0)


def _out_proj_outputs(m, d, tm, final):
    x_spec = pl.BlockSpec((tm, d), lambda i: (i, 0))
    if final:
        return jax.ShapeDtypeStruct((m, d), F32), x_spec
    return ((jax.ShapeDtypeStruct((m, d), F32), jax.ShapeDtypeStruct((m, d), BF16)), (x_spec, x_spec))


def _conformer_in_body(h_ref, wa_ref, wb_ref, wz_ref, v_ref, sz_ref):
    h = h_ref[...]
    a = _dot(h, wa_ref[...])
    b = _dot(h, wb_ref[...])
    z = _dot(h, wz_ref[...])
    v_ref[...] = (a * _sigmoid(b)).astype(v_ref.dtype)
    sz_ref[...] = (z * _sigmoid(z)).astype(sz_ref.dtype)


def _conformer_in(h, w_in, tm=512, tn=512):
    m, d = h.shape
    e = w_in.shape[1] // 3
    nj = e // tn
    h_spec = pl.BlockSpec((tm, d), lambda i, j: (i, 0))
    o_spec = pl.BlockSpec((tm, tn), lambda i, j: (i, j))
    return pl.pallas_call(
        _conformer_in_body,
        out_shape=(jax.ShapeDtypeStruct((m, e), F32), jax.ShapeDtypeStruct((m, e), BF16)),
        grid=(m // tm, nj),
        in_specs=[h_spec,
                  pl.BlockSpec((d, tn), lambda i, j: (0, j)),
                  pl.BlockSpec((d, tn), lambda i, j: (0, j + nj)),
                  pl.BlockSpec((d, tn), lambda i, j: (0, j + 2 * nj))],
        out_specs=(o_spec, o_spec),
        compiler_params=_params("parallel", "parallel"),
        name="conformer_in",
    )(h, w_in, w_in, w_in)


CONV_HALO = 32
CONV_ROWS = 32
CONV_COLS = 512


def _conformer_out_body(tiles_per_seq, final, v_ref, halo_ref, sz_ref, x_ref, dww_ref, dwb_ref, lng_ref, lnb_ref,
                        w_ref, gn_ref, *rest):
    if final:
        xo_ref, vb_ref, y_ref, lhs_ref = rest
        ho_ref = None
    else:
        xo_ref, ho_ref, vb_ref, y_ref, lhs_ref = rest
    tm, e = v_ref.shape
    width = dww_ref.shape[0]
    first = (pl.program_id(0) % tiles_per_seq) == 0
    vb_ref[0:CONV_HALO, :] = jnp.where(first, 0.0, halo_ref[...])
    vb_ref[CONV_HALO:CONV_HALO + tm, :] = v_ref[...]

    def conv_chunk(c, carry):
        r0 = pl.multiple_of(c * CONV_ROWS, CONV_ROWS)
        for cb in range(e // CONV_COLS):
            cs = slice(cb * CONV_COLS, (cb + 1) * CONV_COLS)
            acc = jnp.broadcast_to(dwb_ref[:, cs], (CONV_ROWS, CONV_COLS))
            window = vb_ref[pl.ds(r0, CONV_HALO + CONV_ROWS), cs]
            for k in range(width):
                off = CONV_HALO - (width - 1) + k
                acc = acc + window[off:off + CONV_ROWS, :] * dww_ref[k:k + 1, cs]
            y_ref[pl.ds(r0, CONV_ROWS), cs] = acc
        return carry

    lax.fori_loop(0, tm // CONV_ROWS, conv_chunk, 0)

    def ln_chunk(c, carry):
        r0 = pl.multiple_of(c * BF16_ROWS, BF16_ROWS)
        y = y_ref[pl.ds(r0, BF16_ROWS), :]
        mu = jnp.mean(y, axis=-1, keepdims=True)
        dlt = y - mu
        var = jnp.mean(dlt * dlt, axis=-1, keepdims=True)
        yn = dlt * lax.rsqrt(var + EPS) * lng_ref[...] + lnb_ref[...]
        act = yn * _sigmoid(yn) * sz_ref[pl.ds(r0, BF16_ROWS), :].astype(F32)
        lhs_ref[pl.ds(r0, BF16_ROWS), :] = act.astype(lhs_ref.dtype)
        return carry

    lax.fori_loop(0, tm // BF16_ROWS, ln_chunk, 0)
    _out_proj_tail(lhs_ref, w_ref, x_ref, gn_ref, xo_ref, ho_ref)


def _conformer_out(v, sz, x, dw_w, dw_b, ln_g, ln_b, w_out, g_next, seq, final, tm=256):
    m, e = v.shape
    d = w_out.shape[1]
    width = dw_w.shape[0]
    assert width - 1 <= CONV_HALO and seq % tm == 0 and tm % CONV_HALO == 0
    halo_blocks = tm // CONV_HALO
    row = lambda a: a.reshape(1, -1)
    out_shape, out_specs = _out_proj_outputs(m, d, tm, final)
    return pl.pallas_call(
        functools.partial(_conformer_out_body, seq // tm, final),
        out_shape=out_shape,
        grid=(m // tm,),
        in_specs=[pl.BlockSpec((tm, e), lambda i: (i, 0)),
                  pl.BlockSpec((CONV_HALO, e), lambda i: (jnp.maximum(i * halo_blocks - 1, 0), 0)),
                  pl.BlockSpec((tm, e), lambda i: (i, 0)),
                  pl.BlockSpec((tm, d), lambda i: (i, 0)),
                  pl.BlockSpec((width, e), lambda i: (0, 0)),
                  pl.BlockSpec((1, e), lambda i: (0, 0)),
                  pl.BlockSpec((1, e), lambda i: (0, 0)),
                  pl.BlockSpec((1, e), lambda i: (0, 0)),
                  pl.BlockSpec((e, d), lambda i: (0, 0)),
                  pl.BlockSpec((1, d), lambda i: (0, 0))],
        out_specs=out_specs,
        scratch_shapes=[pltpu.VMEM((CONV_HALO + tm, e), F32), pltpu.VMEM((tm, e), F32), pltpu.VMEM((tm, e), BF16)],
        compiler_params=_params("parallel"),
        name="conformer_out",
    )(v, v, sz, x, dw_w, row(dw_b), row(ln_g), row(ln_b), w_out, row(g_next))


def _matmul_body(h_ref, w_ref, o_ref):
    o_ref[...] = _dot(h_ref[...], w_ref[...]).astype(o_ref.dtype)


def _matmul(h, w, tm=512, tn=512):
    m, d = h.shape
    n = w.shape[1]
    return pl.pallas_call(
        _matmul_body,
        out_shape=jax.ShapeDtypeStruct((m, n), BF16),
        grid=(m // tm, n // tn),
        in_specs=[pl.BlockSpec((tm, d), lambda i, j: (i, 0)), pl.BlockSpec((d, tn), lambda i, j: (0, j))],
        out_specs=pl.BlockSpec((tm, tn), lambda i, j: (i, j)),
        compiler_params=_params("parallel", "parallel"),
        name="nsa_in_z",
    )(h, w)


def _kv_proj_body(h_ref, w_ref, o_ref):
    groups, _, dk = o_ref.shape[1:]
    res = _dot(h_ref[...], w_ref[...]).astype(o_ref.dtype)
    for g in range(groups):
        o_ref[0, g] = res[:, g * dk:(g + 1) * dk]


def _kv_proj(h, w, groups, dk, tm=512):
    m, d = h.shape
    kinds = w.shape[1] // (groups * dk)
    return pl.pallas_call(
        _kv_proj_body,
        out_shape=jax.ShapeDtypeStruct((kinds, groups, m, dk), BF16),
        grid=(m // tm, kinds),
        in_specs=[pl.BlockSpec((tm, d), lambda i, j: (i, 0)), pl.BlockSpec((d, groups * dk), lambda i, j: (0, j))],
        out_specs=pl.BlockSpec((1, groups, tm, dk), lambda i, j: (j, 0, i, 0)),
        compiler_params=_params("parallel", "parallel"),
        name="nsa_in_kv",
    )(h, w)


def _proj_t_body(sigmoid, w_ref, h_ref, o_ref):
    res = _dot_nt(w_ref[...], h_ref[...])
    if sigmoid:
        res = _sigmoid(res)
    res = res.astype(o_ref.dtype)
    for c in range(o_ref.shape[1]):
        o_ref[0, c] = res[:, c * TQ:(c + 1) * TQ]


def _proj_t(h, w_t, batch, seq, out_dtype, sigmoid, name, tm=512):
    m, d = h.shape
    rows = w_t.shape[0]
    tn = next(t for t in (512, 256, 128) if rows % t == 0)
    tiles = seq // tm
    return pl.pallas_call(
        functools.partial(_proj_t_body, sigmoid),
        out_shape=jax.ShapeDtypeStruct((batch, seq // TQ, rows, TQ), out_dtype),
        grid=(m // tm, rows // tn),
        in_specs=[pl.BlockSpec((tn, d), lambda i, j: (j, 0)), pl.BlockSpec((tm, d), lambda i, j: (i, 0))],
        out_specs=pl.BlockSpec((1, tm // TQ, tn, TQ), lambda i, j: (i // tiles, i % tiles, j, 0)),
        compiler_params=_params("parallel", "parallel"),
        name=name,
    )(w_t, h)


def _compress_body(xk_ref, xv_ref, pos_ref, w1k_ref, w2k_ref, w1v_ref, w2v_ref, kc_ref, vct_ref):
    slots = xk_ref.shape[2]

    def hidden(x_ref, w1_ref):
        x = x_ref[0, 0].astype(F32)
        first = _dot((x + pos_ref[0]).astype(BF16), w1_ref[0])
        second = _dot((x + pos_ref[1]).astype(BF16), w1_ref[1])
        pre = first + pltpu.roll(second, slots - 1, 0)
        return (pre * _sigmoid(pre)).astype(BF16)

    kc_ref[0, 0] = _dot(hidden(xk_ref, w1k_ref), w2k_ref[...]).astype(kc_ref.dtype)
    vct_ref[0, 0] = _dot_nt(w2v_ref[...], hidden(xv_ref, w1v_ref)).astype(vct_ref.dtype)


def _compress(kv, pos, w1k, w2k, w1v, w2v, batch, seq):
    kinds, groups, m, dk = kv.shape
    slots = seq // CMP_STRIDE
    assert pos.shape[0] == 2 * CMP_STRIDE
    x = kv.reshape(kinds, groups, m // CMP_STRIDE, CMP_STRIDE * dk)
    pos8 = pos.reshape(2, 1, CMP_STRIDE * dk)
    w1k = w1k.reshape(2, CMP_STRIDE * dk, dk)
    w1v = w1v.reshape(2, CMP_STRIDE * dk, dk)
    full = lambda a: pl.BlockSpec(a.shape, lambda b, g: (0,) * a.ndim)
    return pl.pallas_call(
        _compress_body,
        out_shape=(jax.ShapeDtypeStruct((batch, groups, slots, dk), BF16),
                   jax.ShapeDtypeStruct((batch, groups, dk, slots), BF16)),
        grid=(batch, groups),
        in_specs=[pl.BlockSpec((1, 1, slots, CMP_STRIDE * dk), lambda b, g: (0, g, b, 0)),
                  pl.BlockSpec((1, 1, slots, CMP_STRIDE * dk), lambda b, g: (1, g, b, 0)),
                  full(pos8), full(w1k), full(w2k), full(w1v), full(w2v)],
        out_specs=(pl.BlockSpec((1, 1, slots, dk), lambda b, g: (b, g, 0, 0)),
                   pl.BlockSpec((1, 1, dk, slots), lambda b, g: (b, g, 0, 0))),
        compiler_params=_params("parallel", "parallel"),
        name="nsa_compress",
    )(x, x, pos8, w1k, w2k, w1v, w2v)


def _t5_bucket_np(dist, num_buckets):
    dist = np.maximum(dist, 0)
    max_exact = num_buckets // 2
    d = np.maximum(dist, max_exact).astype(np.float32)
    ratio = np.log(d / np.float32(max_exact)) / np.float32(math.log(MAX_DISTANCE / max_exact))
    large = max_exact + (ratio * np.float32(num_buckets - max_exact)).astype(np.int32)
    large = np.minimum(large, num_buckets - 1)
    return np.where(dist < max_exact, dist, large).astype(np.int32)


def _table_layout(seq):
    slots = seq // CMP_STRIDE
    cmp_rows = slots + (seq // TQ - 1) * (TQ // CMP_STRIDE)
    cmp_rows = -(-cmp_rows // 64) * 64
    return dict(win_far=0, far=TQ, prev=2 * TQ, diag=3 * TQ, cmp=4 * TQ, rows=4 * TQ + cmp_rows, cmp_rows=cmp_rows)


def _bucket_index_tables(seq, num_buckets, cmp_len):
    lay = _table_layout(seq)
    j = np.arange(TQ)[:, None]
    i = np.arange(TQ)[None, :]
    far_bucket = _t5_bucket_np(np.full((TQ, TQ), 2 * TQ), num_buckets)
    assert TQ >= MAX_DISTANCE
    win_far = np.where(i < j, far_bucket, -1)
    far = far_bucket
    prev = _t5_bucket_np(i - j + TQ, num_buckets)
    diag = np.where(i >= j, _t5_bucket_np(i - j, num_buckets), -1)
    rho = np.arange(lay["cmp_rows"])[:, None]
    rel_slot = rho - (seq // TQ - 1) * (TQ // CMP_STRIDE)
    dist_c = i - CMP_STRIDE * rel_slot - (cmp_len - 1)
    cmp = np.where(dist_c >= 0, _t5_bucket_np(dist_c, num_buckets), -1)
    return np.concatenate([win_far, far, prev, diag, cmp], axis=0).astype(np.int32)


TABLE_CHUNK = 64


def _bias_table_body(rb_ref, idx_ref, o_ref):
    head = pl.program_id(0)
    num_buckets = rb_ref.shape[0]

    def chunk(c, carry):
        r0 = pl.multiple_of(c * TABLE_CHUNK, TABLE_CHUNK)
        idx = idx_ref[pl.ds(r0, TABLE_CHUNK), :]
        acc = jnp.full(idx.shape, NEG, F32)
        for k in range(num_buckets):
            acc = jnp.where(idx == k, rb_ref[k, head], acc)
        o_ref[0, pl.ds(r0, TABLE_CHUNK), :] = acc
        return carry

    lax.fori_loop(0, idx_ref.shape[0] // TABLE_CHUNK, chunk, 0)


def _bias_tables(rel_bias, seq, cmp_len):
    num_buckets, heads = rel_bias.shape
    idx = jnp.asarray(_bucket_index_tables(seq, num_buckets, cmp_len))
    rows = idx.shape[0]
    return pl.pallas_call(
        _bias_table_body,
        out_shape=jax.ShapeDtypeStruct((heads, rows, TQ), F32),
        grid=(heads,),
        in_specs=[pl.BlockSpec(memory_space=pltpu.SMEM), pl.BlockSpec((rows, TQ), lambda h: (0, 0))],
        out_specs=pl.BlockSpec((1, rows, TQ), lambda h: (h, 0, 0)),
        compiler_params=_params("parallel"),
        name="t5_bias_tables",
    )(rel_bias, idx)


def _attention_body(seq, lay, qt_ref, kc_ref, vct_ref, ks_ref, kw_ref, vst_ref, vwt_ref, gt_ref, tab_ref, ovl_ref,
                    o_ref, oc_ref, selb_ref):
    qb = pl.program_id(2)
    heads_per_group = tab_ref.shape[0]
    dk = kc_ref.shape[3]
    slots = kc_ref.shape[2]
    n_blocks = ovl_ref.shape[0]
    n_tiles = seq // TQ
    slots_per_tile = TQ // CMP_STRIDE
    blocks_per_tile = TQ // SEL_LEN
    scale = dk ** -0.5
    sel_shift = SEL_LEN.bit_length() - 1

    def q_of(r):
        return qt_ref[0, 0, r * dk:(r + 1) * dk, :]

    cmp_row0 = lay["cmp"] + pl.multiple_of((n_tiles - 1 - qb) * slots_per_tile, slots_per_tile)
    kc = kc_ref[0, 0]
    vct = vct_ref[0, 0]
    p_sum = jnp.zeros((slots, TQ), F32)
    for r in range(heads_per_group):
        s = _dot(kc, q_of(r)) * scale + tab_ref[r, pl.ds(cmp_row0, slots), :]
        valid = s > 0.5 * NEG
        m = jnp.max(s, axis=0, keepdims=True)
        p = jnp.where(valid, jnp.exp(s - m), 0.0)
        l = jnp.sum(p, axis=0, keepdims=True)
        p = p * jnp.where(l > 0.0, 1.0 / l, 0.0)
        p_sum = p_sum + p
        oc_ref[r] = _dot(vct, p.astype(BF16))

    p_hi = p_sum.astype(BF16)
    p_lo = (p_sum - p_hi.astype(F32)).astype(BF16)
    ovl = ovl_ref[...]
    imp = _dot(ovl, p_hi) + _dot(ovl, p_lo)
    blk = lax.broadcasted_iota(jnp.int32, (n_blocks, TQ), 0)
    tq = qb * TQ + lax.broadcasted_iota(jnp.int32, (n_blocks, TQ), 1)
    cur = tq >> sel_shift
    forced = (blk == 0) | (blk == cur) | (blk == cur - 1)
    imp = jnp.where(forced, FORCE, imp)
    imp = jnp.where((blk << sel_shift) <= tq, imp, NEG)
    selb = jnp.full((n_blocks, TQ), NEG, F32)
    for _ in range(min(N_SELECT, n_blocks)):
        top = jnp.max(imp, axis=0, keepdims=True)
        first = jnp.min(jnp.where(imp == top, blk, n_blocks), axis=0, keepdims=True)
        pick = blk == first
        selb = jnp.where(pick, 0.0, selb)
        imp = jnp.where(pick, -jnp.inf, imp)
    for b in range(n_blocks):
        selb_ref[b] = jnp.broadcast_to(selb[b:b + 1, :], (SUBLANES, TQ))

    def online_step(k, v_t, bias, carry):
        m, l, acc = carry
        s = bias(_dot(k, q_cur[0]) * scale)
        m_new = jnp.maximum(m, jnp.max(s, axis=0, keepdims=True))
        alpha = jnp.exp(m - m_new)
        p = jnp.exp(s - m_new)
        l_new = alpha * l + jnp.sum(p, axis=0, keepdims=True)
        acc_new = alpha * acc + _dot(v_t, p.astype(BF16))
        return m_new, l_new, acc_new

    def key_rows(ref, kt):
        return ref[0, 0, pl.ds(pl.multiple_of(kt * TQ, TQ), TQ), :]

    q_cur = [None]
    gates = gt_ref[0, 0]
    for r in range(heads_per_group):
        q_cur[0] = q_of(r)
        init = (jnp.full((1, TQ), NEG, F32), jnp.zeros((1, TQ), F32), jnp.zeros((dk, TQ), F32))

        def sel_step(kt, carry, r=r):
            tile = jnp.clip(kt - qb + 2, 0, 2)
            row0 = pl.multiple_of(lay["far"] + tile * TQ, TQ)

            def bias(s):
                mask = jnp.concatenate(
                    [jnp.tile(selb_ref[kt * blocks_per_tile + c], (SEL_LEN // SUBLANES, 1))
                     for c in range(blocks_per_tile)], axis=0)
                return s + tab_ref[r, pl.ds(row0, TQ), :] + mask

            return online_step(key_rows(ks_ref, kt), vst_ref[0, kt], bias, carry)

        _, l_s, acc_s = lax.fori_loop(0, qb + 1, sel_step, init)

        def win_step(it, carry, r=r):
            kt = qb - it
            row0 = pl.multiple_of(jnp.where(it == 0, lay["diag"], jnp.where(it == 1, lay["prev"], lay["win_far"])), TQ)
            return online_step(key_rows(kw_ref, kt), vwt_ref[0, kt], lambda s: s + tab_ref[r, pl.ds(row0, TQ), :],
                               carry)

        _, l_w, acc_w = lax.fori_loop(0, jnp.minimum(qb, WINDOW // TQ) + 1, win_step, init)

        g_c = gates[3 * r:3 * r + 1, :]
        g_s = gates[3 * r + 1:3 * r + 2, :]
        g_w = gates[3 * r + 2:3 * r + 3, :]
        out_t = g_c * oc_ref[r] + (g_s / l_s) * acc_s + (g_w / l_w) * acc_w
        o_ref[0, :, r * dk:(r + 1) * dk] = out_t.T.astype(o_ref.dtype)


def _gate_rows_per_group():
    return -(-3 * GROUP_SIZE // SUBLANES) * SUBLANES


def _attention(proj_t, gates_t, kv, kc, vct, tables, batch, seq, heads, groups, dk, lay):
    assert WINDOW == 2 * TQ and TQ % SEL_LEN == 0 and seq % TQ == 0
    r = heads // groups
    n_tiles = seq // TQ
    slots = seq // CMP_STRIDE
    n_blocks = seq // SEL_LEN
    q_rows = heads * dk
    kv_rows = groups * dk
    gate_rows = _gate_rows_per_group()
    n0 = np.arange(slots)[None, :] * CMP_STRIDE
    s0 = np.arange(n_blocks)[:, None] * SEL_LEN
    ovl = jnp.asarray(((n0 < s0 + SEL_LEN) & (n0 + 2 * CMP_STRIDE > s0) & (n0 + 2 * CMP_STRIDE <= seq)), dtype=BF16)
    vs_block0 = q_rows // dk
    vw_block0 = (q_rows + kv_rows) // dk
    return pl.pallas_call(
        functools.partial(_attention_body, seq, lay),
        out_shape=jax.ShapeDtypeStruct((batch, seq, q_rows), BF16),
        grid=(batch, groups, n_tiles),
        in_specs=[
            pl.BlockSpec((1, 1, r * dk, TQ), lambda b, g, t: (b, t, g, 0)),
            pl.BlockSpec((1, 1, slots, dk), lambda b, g, t: (b, g, 0, 0)),
            pl.BlockSpec((1, 1, dk, slots), lambda b, g, t: (b, g, 0, 0)),
            pl.BlockSpec((1, 1, seq, dk), lambda b, g, t: (2, g, b, 0)),
            pl.BlockSpec((1, 1, seq, dk), lambda b, g, t: (3, g, b, 0)),
            pl.BlockSpec((1, n_tiles, dk, TQ), lambda b, g, t: (b, 0, vs_block0 + g, 0)),
            pl.BlockSpec((1, n_tiles, dk, TQ), lambda b, g, t: (b, 0, vw_block0 + g, 0)),
            pl.BlockSpec((1, 1, gate_rows, TQ), lambda b, g, t: (b, t, g, 0)),
            pl.BlockSpec((r, lay["rows"], TQ), lambda b, g, t: (g, 0, 0)),
            pl.BlockSpec((n_blocks, slots), lambda b, g, t: (0, 0)),
        ],
        out_specs=pl.BlockSpec((1, TQ, r * dk), lambda b, g, t: (b, t, g)),
        scratch_shapes=[pltpu.VMEM((r, dk, TQ), F32), pltpu.VMEM((n_blocks, SUBLANES, TQ), F32)],
        compiler_params=_params("parallel", "parallel", "arbitrary"),
        name="nsa_attention",
    )(proj_t, kc, vct, kv, kv, proj_t, proj_t, gates_t, tables, ovl)


def _nsa_out_body(final, o_ref, z_ref, x_ref, w_ref, gn_ref, *rest):
    if final:
        xo_ref, lhs_ref = rest
        ho_ref = None
    else:
        xo_ref, ho_ref, lhs_ref = rest
    tm = o_ref.shape[0]

    def chunk(c, carry):
        r0 = pl.multiple_of(c * BF16_ROWS, BF16_ROWS)
        z = z_ref[pl.ds(r0, BF16_ROWS), :].astype(F32)
        o = o_ref[pl.ds(r0, BF16_ROWS), :].astype(F32)
        lhs_ref[pl.ds(r0, BF16_ROWS), :] = (o * (z * _sigmoid(z))).astype(lhs_ref.dtype)
        return carry

    lax.fori_loop(0, tm // BF16_ROWS, chunk, 0)
    _out_proj_tail(lhs_ref, w_ref, x_ref, gn_ref, xo_ref, ho_ref)


def _nsa_out(o, z, x, w_out, g_next, final, tm=256):
    m, e = o.shape
    d = w_out.shape[1]
    out_shape, out_specs = _out_proj_outputs(m, d, tm, final)
    return pl.pallas_call(
        functools.partial(_nsa_out_body, final),
        out_shape=out_shape,
        grid=(m // tm,),
        in_specs=[pl.BlockSpec((tm, e), lambda i: (i, 0)),
                  pl.BlockSpec((tm, e), lambda i: (i, 0)),
                  pl.BlockSpec((tm, d), lambda i: (i, 0)),
                  pl.BlockSpec((e, d), lambda i: (0, 0)),
                  pl.BlockSpec((1, d), lambda i: (0, 0))],
        out_specs=out_specs,
        scratch_shapes=[pltpu.VMEM((tm, e), BF16)],
        compiler_params=_params("parallel"),
        name="nsa_out",
    )(o, z, x, w_out, g_next.reshape(1, d))


def _conformer_layer(x, h, seq, w_in, dw_w, dw_b, ln_g, ln_b, w_out, g_next, final):
    v, sz = _conformer_in(h, w_in.astype(BF16))
    return _conformer_out(v, sz, x, dw_w, dw_b, ln_g, ln_b, w_out.astype(BF16), g_next, seq, final)


def _nsa_layer(x, h, batch, seq, heads, tables, lay, w_in, cmp_pos, ck_w1, ck_w2, cv_w1, cv_w2, w_out, g_next, final):
    d = x.shape[1]
    dk = d // heads
    groups = heads // GROUP_SIZE
    q_w = heads * dk
    kv_w = groups * dk
    gate_w = 3 * heads
    assert w_in.shape[1] == 2 * q_w + 6 * kv_w + gate_w
    cuts = np.cumsum([0, q_w] + [kv_w] * 6 + [gate_w, q_w])
    col = lambda k: w_in[:, cuts[k]:cuts[k + 1]]
    w_q, w_kc, w_vc, w_ks, w_vs, w_kw, w_vw, w_g, w_z = (col(k) for k in range(9))
    w_kv = jnp.concatenate([w_kc, w_vc, w_ks, w_kw], axis=1).astype(BF16)
    w_t = jnp.concatenate([w_q, w_vs, w_vw], axis=1).T.astype(BF16)
    per_group = 3 * GROUP_SIZE
    gate_rows = _gate_rows_per_group()
    w_g =jnp.pad(w_g.reshape(d, groups, per_group), ((0, 0), (0, 0), (0, gate_rows - per_group)))
    w_g_t = jnp.pad(w_g.reshape(d, groups * gate_rows).T, ((0, -groups * gate_rows % 128), (0, 0))).astype(BF16)

    kv = _kv_proj(h, w_kv, groups, dk)
    z = _matmul(h, w_z.astype(BF16))
    proj_t = _proj_t(h, w_t, batch, seq, BF16, False, "nsa_in_t")
    gates_t = _proj_t(h, w_g_t, batch, seq, F32, True, "nsa_in_gates")
    kc, vct = _compress(kv, cmp_pos, ck_w1.astype(BF16), ck_w2.astype(BF16), cv_w1.astype(BF16),
                        cv_w2.T.astype(BF16), batch, seq)
    o = _attention(proj_t, gates_t, kv, kc, vct, tables, batch, seq, heads, groups, dk, lay)
    return _nsa_out(o.reshape(batch * seq, q_w), z, x, w_out.astype(BF16), g_next, final)


def kernel(x, rel_bias, l0_norm, l0_w_in, l0_dw_w, l0_dw_b, l0_ln_g, l0_ln_b, l0_w_out, l1_norm, l1_w_in, l1_cmp_pos, l1_ck_w1, l1_ck_w2, l1_cv_w1, l1_cv_w2, l1_w_out, l2_norm, l2_w_in, l2_dw_w, l2_dw_b, l2_ln_g, l2_ln_b, l2_w_out, l3_norm, l3_w_in, l3_cmp_pos, l3_ck_w1, l3_ck_w2, l3_cv_w1, l3_cv_w2, l3_w_out, final_norm):
    batch, seq, d = x.shape
    heads = rel_bias.shape[1]
    lay = _table_layout(seq)
    tables = _bias_tables(rel_bias, seq, l1_cmp_pos.shape[0])
    x2 = x.reshape(batch * seq, d)
    h = _rmsnorm(x2, l0_norm)
    x2, h = _conformer_layer(x2, h, seq, l0_w_in, l0_dw_w, l0_dw_b, l0_ln_g, l0_ln_b, l0_w_out, l1_norm, False)
    x2, h = _nsa_layer(x2, h, batch, seq, heads, tables, lay, l1_w_in, l1_cmp_pos, l1_ck_w1, l1_ck_w2, l1_cv_w1,
                       l1_cv_w2, l1_w_out, l2_norm, False)
    x2, h = _conformer_layer(x2, h, seq, l2_w_in, l2_dw_w, l2_dw_b, l2_ln_g, l2_ln_b, l2_w_out, l3_norm, False)
    y = _nsa_layer(x2, h, batch, seq, heads, tables, lay, l3_w_in, l3_cmp_pos, l3_ck_w1, l3_ck_w2, l3_cv_w1,
                   l3_cv_w2, l3_w_out, final_norm, True)
    return y.reshape(batch, seq, d)
```

```python
import functools
import math

import jax
import jax.numpy as jnp
import numpy as np
from jax import lax
from jax.experimental import pallas as pl
from jax.experimental.pallas import tpu as pltpu

F32 = jnp.float32
BF16 = jnp.bfloat16

GROUP_SIZE = 4
CMP_STRIDE = 16
SEL_LEN = 64
N_SELECT = 8
WINDOW = 512
MAX_DISTANCE = 128
EPS = 1e-6
NEG = -1e30
FORCE = 1e6

TQ = 256
KEY_CHUNK = 64
V7X_VMEM_LIMIT_BYTES = 56 * 1024 * 1024
SUBLANES = 8
BF16_ROWS = 16


def _params(*semantics):
    return pltpu.CompilerParams(dimension_semantics=semantics, vmem_limit_bytes=V7X_VMEM_LIMIT_BYTES)


def _sigmoid(x):
    return 1.0 / (1.0 + jnp.exp(-x))


def _dot(a, b):
    return jnp.dot(a, b, preferred_element_type=F32)


def _dot_nt(a, b):
    return lax.dot_general(a, b, (((1,), (1,)), ((), ())), preferred_element_type=F32)


def _rmsnorm_body(x_ref, g_ref, o_ref):
    x = x_ref[...]
    ms = jnp.mean(x * x, axis=-1, keepdims=True)
    o_ref[...] = (x * lax.rsqrt(ms + EPS) * g_ref[...]).astype(o_ref.dtype)


def _rmsnorm(x2d, gain, tm=512):
    m, d = x2d.shape
    return pl.pallas_call(
        _rmsnorm_body,
        out_shape=jax.ShapeDtypeStruct((m, d), BF16),
        grid=(m // tm,),
        in_specs=[pl.BlockSpec((tm, d), lambda i: (i, 0)), pl.BlockSpec((1, d), lambda i: (0, 0))],
        out_specs=pl.BlockSpec((tm, d), lambda i: (i, 0)),
        compiler_params=_params("parallel"),
        name="rmsnorm",
    )(x2d, gain.reshape(1, d))


def _out_proj_tail(lhs_ref, w_ref, x_ref, gn_ref, xo_ref, ho_ref):
    tm = xo_ref.shape[0]
    xo_ref[...] = x_ref[...] + _dot(lhs_ref[...], w_ref[...])

    def chunk(c, carry):
        r0 = pl.multiple_of(c * BF16_ROWS, BF16_ROWS)
        xn = xo_ref[pl.ds(r0, BF16_ROWS), :]
        ms = jnp.mean(xn * xn, axis=-1, keepdims=True)
        hn = xn * lax.rsqrt(ms + EPS) * gn_ref[...]
        if ho_ref is None:
            xo_ref[pl.ds(r0, BF16_ROWS), :] = hn
        else:
            ho_ref[pl.ds(r0, BF16_ROWS), :] = hn.astype(ho_ref.dtype)
        return carry

    lax.fori_loop(0, tm // BF16_ROWS, chunk, 0)


def _out_proj_outputs(m, d, tm, final):
    x_spec = pl.BlockSpec((tm, d), lambda i: (i, 0))
    if final:
        return jax.ShapeDtypeStruct((m, d), F32), x_spec
    return ((jax.ShapeDtypeStruct((m, d), F32), jax.ShapeDtypeStruct((m, d), BF16)), (x_spec, x_spec))


def _conformer_in_body(h_ref, wa_ref, wb_ref, wz_ref, v_ref, sz_ref):
    h = h_ref[...]
    a = _dot(h, wa_ref[...])
    b = _dot(h, wb_ref[...])
    z = _dot(h, wz_ref[...])
    v_ref[...] = (a * _sigmoid(b)).astype(v_ref.dtype)
    sz_ref[...] = (z * _sigmoid(z)).astype(sz_ref.dtype)


def _conformer_in(h, w_in, tm=512, tn=512):
    m, d = h.shape
    e = w_in.shape[1] // 3
    nj = e // tn
    h_spec = pl.BlockSpec((tm, d), lambda i, j: (i, 0))
    o_spec = pl.BlockSpec((tm, tn), lambda i, j: (i, j))
    return pl.pallas_call(
        _conformer_in_body,
        out_shape=(jax.ShapeDtypeStruct((m, e), F32), jax.ShapeDtypeStruct((m, e), BF16)),
        grid=(m // tm, nj),
        in_specs=[h_spec,
                  pl.BlockSpec((d, tn), lambda i, j: (0, j)),
                  pl.BlockSpec((d, tn), lambda i, j: (0, j + nj)),
                  pl.BlockSpec((d, tn), lambda i, j: (0, j + 2 * nj))],
        out_specs=(o_spec, o_spec),
        compiler_params=_params("parallel", "parallel"),
        name="conformer_in",
    )(h, w_in, w_in, w_in)


CONV_HALO = 32
CONV_ROWS = 32
CONV_COLS = 512


def _conformer_out_body(tiles_per_seq, final, v_ref, halo_ref, sz_ref, x_ref, dww_ref, dwb_ref, lng_ref, lnb_ref,
                        w_ref, gn_ref, *rest):
    if final:
        xo_ref, vb_ref, y_ref, lhs_ref = rest
        ho_ref = None
    else:
        xo_ref, ho_ref, vb_ref, y_ref, lhs_ref = rest
    tm, e = v_ref.shape
    width = dww_ref.shape[0]
    first = (pl.program_id(0) % tiles_per_seq) == 0
    vb_ref[0:CONV_HALO, :] = jnp.where(first, 0.0, halo_ref[...])
    vb_ref[CONV_HALO:CONV_HALO + tm, :] = v_ref[...]

    def conv_chunk(c, carry):
        r0 = pl.multiple_of(c * CONV_ROWS, CONV_ROWS)
        for cb in range(e // CONV_COLS):
            cs = slice(cb * CONV_COLS, (cb + 1) * CONV_COLS)
            acc = jnp.broadcast_to(dwb_ref[:, cs], (CONV_ROWS, CONV_COLS))
            window = vb_ref[pl.ds(r0, CONV_HALO + CONV_ROWS), cs]
            for k in range(width):
                off = CONV_HALO - (width - 1) + k
                acc = acc + window[off:off + CONV_ROWS, :] * dww_ref[k:k + 1, cs]
            y_ref[pl.ds(r0, CONV_ROWS), cs] = acc
        return carry

    lax.fori_loop(0, tm // CONV_ROWS, conv_chunk, 0)

    def ln_chunk(c, carry):
        r0 = pl.multiple_of(c * BF16_ROWS, BF16_ROWS)
        y = y_ref[pl.ds(r0, BF16_ROWS), :]
        mu = jnp.mean(y, axis=-1, keepdims=True)
        dlt = y - mu
        var = jnp.mean(dlt * dlt, axis=-1, keepdims=True)
        yn = dlt * lax.rsqrt(var + EPS) * lng_ref[...] + lnb_ref[...]
        act = yn * _sigmoid(yn) * sz_ref[pl.ds(r0, BF16_ROWS), :].astype(F32)
        lhs_ref[pl.ds(r0, BF16_ROWS), :] = act.astype(lhs_ref.dtype)
        return carry

    lax.fori_loop(0, tm // BF16_ROWS, ln_chunk, 0)
    _out_proj_tail(lhs_ref, w_ref, x_ref, gn_ref, xo_ref, ho_ref)


def _conformer_out(v, sz, x, dw_w, dw_b, ln_g, ln_b, w_out, g_next, seq, final, tm=256):
    m, e = v.shape
    d = w_out.shape[1]
    width = dw_w.shape[0]
    assert width - 1 <= CONV_HALO and seq % tm == 0 and tm % CONV_HALO == 0
    halo_blocks = tm // CONV_HALO
    row = lambda a: a.reshape(1, -1)
    out_shape, out_specs = _out_proj_outputs(m, d, tm, final)
    return pl.pallas_call(
        functools.partial(_conformer_out_body, seq // tm, final),
        out_shape=out_shape,
        grid=(m // tm,),
        in_specs=[pl.BlockSpec((tm, e), lambda i: (i, 0)),
                  pl.BlockSpec((CONV_HALO, e), lambda i: (jnp.maximum(i * halo_blocks - 1, 0), 0)),
                  pl.BlockSpec((tm, e), lambda i: (i, 0)),
                  pl.BlockSpec((tm, d), lambda i: (i, 0)),
                  pl.BlockSpec((width, e), lambda i: (0, 0)),
                  pl.BlockSpec((1, e), lambda i: (0, 0)),
                  pl.BlockSpec((1, e), lambda i: (0, 0)),
                  pl.BlockSpec((1, e), lambda i: (0, 0)),
                  pl.BlockSpec((e, d), lambda i: (0, 0)),
                  pl.BlockSpec((1, d), lambda i: (0, 0))],
        out_specs=out_specs,
        scratch_shapes=[pltpu.VMEM((CONV_HALO + tm, e), F32), pltpu.VMEM((tm, e), F32), pltpu.VMEM((tm, e), BF16)],
        compiler_params=_params("parallel"),
        name="conformer_out",
    )(v, v, sz, x, dw_w, row(dw_b), row(ln_g), row(ln_b), w_out, row(g_next))


def _matmul_body(h_ref, w_ref, o_ref):
    o_ref[...] = _dot(h_ref[...], w_ref[...]).astype(o_ref.dtype)


def _matmul(h, w, tm=512, tn=512):
    m, d = h.shape
    n = w.shape[1]
    return pl.pallas_call(
        _matmul_body,
        out_shape=jax.ShapeDtypeStruct((m, n), BF16),
        grid=(m // tm, n // tn),
        in_specs=[pl.BlockSpec((tm, d), lambda i, j: (i, 0)), pl.BlockSpec((d, tn), lambda i, j: (0, j))],
        out_specs=pl.BlockSpec((tm, tn), lambda i, j: (i, j)),
        compiler_params=_params("parallel", "parallel"),
        name="nsa_in_z",
    )(h, w)


def _kv_proj_body(h_ref, w_ref, o_ref):
    groups, _, dk = o_ref.shape[1:]
    res = _dot(h_ref[...], w_ref[...]).astype(o_ref.dtype)
    for g in range(groups):
        o_ref[0, g] = res[:, g * dk:(g + 1) * dk]


def _kv_proj(h, w, groups, dk, tm=512):
    m, d = h.shape
    kinds = w.shape[1] // (groups * dk)
    return pl.pallas_call(
        _kv_proj_body,
        out_shape=jax.ShapeDtypeStruct((kinds, groups, m, dk), BF16),
        grid=(m // tm, kinds),
        in_specs=[pl.BlockSpec((tm, d), lambda i, j: (i, 0)), pl.BlockSpec((d, groups * dk), lambda i, j: (0, j))],
        out_specs=pl.BlockSpec((1, groups, tm, dk), lambda i, j: (j, 0, i, 0)),
        compiler_params=_params("parallel", "parallel"),
        name="nsa_in_kv",
    )(h, w)


def _proj_t_body(sigmoid, w_ref, h_ref, o_ref):
    res = _dot_nt(w_ref[...], h_ref[...])
    if sigmoid:
        res = _sigmoid(res)
    res = res.astype(o_ref.dtype)
    for c in range(o_ref.shape[1]):
        o_ref[0, c] = res[:, c * TQ:(c + 1) * TQ]


def _proj_t(h, w_t, batch, seq, out_dtype, sigmoid, name, tm=512):
    m, d = h.shape
    rows = w_t.shape[0]
    tn = next(t for t in (512, 256, 128) if rows % t == 0)
    tiles = seq // tm
    return pl.pallas_call(
        functools.partial(_proj_t_body, sigmoid),
        out_shape=jax.ShapeDtypeStruct((batch, seq // TQ, rows, TQ), out_dtype),
        grid=(m // tm, rows // tn),
        in_specs=[pl.BlockSpec((tn, d), lambda i, j: (j, 0)), pl.BlockSpec((tm, d), lambda i, j: (i, 0))],
        out_specs=pl.BlockSpec((1, tm // TQ, tn, TQ), lambda i, j: (i // tiles, i % tiles, j, 0)),
        compiler_params=_params("parallel", "parallel"),
        name=name,
    )(w_t, h)


def _compress_body(xk_ref, xv_ref, pos_ref, w1k_ref, w2k_ref, w1v_ref, w2v_ref, kc_ref, vct_ref):
    slots = xk_ref.shape[2]

    def hidden(x_ref, w1_ref):
        x = x_ref[0, 0].astype(F32)
        first = _dot((x + pos_ref[0]).astype(BF16), w1_ref[0])
        second = _dot((x + pos_ref[1]).astype(BF16), w1_ref[1])
        pre = first + pltpu.roll(second, slots - 1, 0)
        return (pre * _sigmoid(pre)).astype(BF16)

    kc_ref[0, 0] = _dot(hidden(xk_ref, w1k_ref), w2k_ref[...]).astype(kc_ref.dtype)
    vct_ref[0, 0] = _dot_nt(w2v_ref[...], hidden(xv_ref, w1v_ref)).astype(vct_ref.dtype)


def _compress(kv, pos, w1k, w2k, w1v, w2v, batch, seq):
    kinds, groups, m, dk = kv.shape
    slots = seq // CMP_STRIDE
    assert pos.shape[0] == 2 * CMP_STRIDE
    x = kv.reshape(kinds, groups, m // CMP_STRIDE, CMP_STRIDE * dk)
    pos8 = pos.reshape(2, 1, CMP_STRIDE * dk)
    w1k = w1k.reshape(2, CMP_STRIDE * dk, dk)
    w1v = w1v.reshape(2, CMP_STRIDE * dk, dk)
    full = lambda a: pl.BlockSpec(a.shape, lambda b, g: (0,) * a.ndim)
    return pl.pallas_call(
        _compress_body,
        out_shape=(jax.ShapeDtypeStruct((batch, groups, slots, dk), BF16),
                   jax.ShapeDtypeStruct((batch, groups, dk, slots), BF16)),
        grid=(batch, groups),
        in_specs=[pl.BlockSpec((1, 1, slots, CMP_STRIDE * dk), lambda b, g: (0, g, b, 0)),
                  pl.BlockSpec((1, 1, slots, CMP_STRIDE * dk), lambda b, g: (1, g, b, 0)),
                  full(pos8), full(w1k), full(w2k), full(w1v), full(w2v)],
        out_specs=(pl.BlockSpec((1, 1, slots, dk), lambda b, g: (b, g, 0, 0)),
                   pl.BlockSpec((1, 1, dk, slots), lambda b, g: (b, g, 0, 0))),
        compiler_params=_params("parallel", "parallel"),
        name="nsa_compress",
    )(x, x, pos8, w1k, w2k, w1v, w2v)


def _t5_bucket_np(dist, num_buckets):
    dist = np.maximum(dist, 0)
    max_exact = num_buckets // 2
    d = np.maximum(dist, max_exact).astype(np.float32)
    ratio = np.log(d / np.float32(max_exact)) / np.float32(math.log(MAX_DISTANCE / max_exact))
    large = max_exact + (ratio * np.float32(num_buckets - max_exact)).astype(np.int32)
    large = np.minimum(large, num_buckets - 1)
    return np.where(dist < max_exact, dist, large).astype(np.int32)


def _table_layout(seq):
    slots = seq // CMP_STRIDE
    cmp_rows = slots + (seq // TQ - 1) * (TQ // CMP_STRIDE)
    cmp_rows = -(-cmp_rows // 64) * 64
    return dict(win_far=0, far=TQ, prev=2 * TQ, diag=3 * TQ, cmp=4 * TQ, rows=4 * TQ + cmp_rows, cmp_rows=cmp_rows)


def _bucket_index_tables(seq, num_buckets, cmp_len):
    lay = _table_layout(seq)
    j = np.arange(TQ)[:, None]
    i = np.arange(TQ)[None, :]
    far_bucket = _t5_bucket_np(np.full((TQ, TQ), 2 * TQ), num_buckets)
    assert TQ >= MAX_DISTANCE
    win_far = np.where(i < j, far_bucket, -1)
    far = far_bucket
    prev = _t5_bucket_np(i - j + TQ, num_buckets)
    diag = np.where(i >= j, _t5_bucket_np(i - j, num_buckets), -1)
    rho = np.arange(lay["cmp_rows"])[:, None]
    rel_slot = rho - (seq // TQ - 1) * (TQ // CMP_STRIDE)
    dist_c = i - CMP_STRIDE * rel_slot - (cmp_len - 1)
    cmp = np.where(dist_c >= 0, _t5_bucket_np(dist_c, num_buckets), -1)
    return np.concatenate([win_far, far, prev, diag, cmp], axis=0).astype(np.int32)


TABLE_CHUNK = 64


def _bias_table_body(rb_ref, idx_ref, o_ref):
    head = pl.program_id(0)
    num_buckets = rb_ref.shape[0]

    def chunk(c, carry):
        r0 = pl.multiple_of(c * TABLE_CHUNK, TABLE_CHUNK)
        idx = idx_ref[pl.ds(r0, TABLE_CHUNK), :]
        acc = jnp.full(idx.shape, NEG, F32)
        for k in range(num_buckets):
            acc = jnp.where(idx == k, rb_ref[k, head], acc)
        o_ref[0, pl.ds(r0, TABLE_CHUNK), :] = acc
        return carry

    lax.fori_loop(0, idx_ref.shape[0] // TABLE_CHUNK, chunk, 0)


def _bias_tables(rel_bias, seq, cmp_len):
    num_buckets, heads = rel_bias.shape
    idx = jnp.asarray(_bucket_index_tables(seq, num_buckets, cmp_len))
    rows = idx.shape[0]
    return pl.pallas_call(
        _bias_table_body,
        out_shape=jax.ShapeDtypeStruct((heads, rows, TQ), F32),
        grid=(heads,),
        in_specs=[pl.BlockSpec(memory_space=pltpu.SMEM), pl.BlockSpec((rows, TQ), lambda h: (0, 0))],
        out_specs=pl.BlockSpec((1, rows, TQ), lambda h: (h, 0, 0)),
        compiler_params=_params("parallel"),
        name="t5_bias_tables",
    )(rel_bias, idx)


def _attention_body(seq, lay, qt_ref, kc_ref, vct_ref, ks_ref, kw_ref, vst_ref, vwt_ref, gt_ref, tab_ref, ovl_ref,
                    o_ref, oc_ref, accs_ref, accw_ref, selb_ref, s_all_ref, p_all_ref):
    qb = pl.program_id(2)
    heads_per_group = tab_ref.shape[0]
    dk = kc_ref.shape[3]
    slots = kc_ref.shape[2]
    n_blocks = ovl_ref.shape[0]
    n_tiles = seq // TQ
    slots_per_tile = TQ // CMP_STRIDE
    blocks_per_tile = TQ // SEL_LEN
    scale = dk ** -0.5
    sel_shift = SEL_LEN.bit_length() - 1

    def q_of(r):
        return qt_ref[0, 0, r * dk:(r + 1) * dk, :]

    cmp_row0 = lay["cmp"] + pl.multiple_of((n_tiles - 1 - qb) * slots_per_tile, slots_per_tile)
    kc = kc_ref[0, 0]
    vct = vct_ref[0, 0]
    p_sum = jnp.zeros((slots, TQ), F32)
    for r in range(heads_per_group):
        s = _dot(kc, q_of(r)) * scale + tab_ref[r, pl.ds(cmp_row0, slots), :]
        valid = s > 0.5 * NEG
        m = jnp.max(s, axis=0, keepdims=True)
        p = jnp.where(valid, jnp.exp(s - m), 0.0)
        l = jnp.sum(p, axis=0, keepdims=True)
        p = p * jnp.where(l > 0.0, 1.0 / l, 0.0)
        p_sum = p_sum + p
        oc_ref[r] = _dot(vct, p.astype(BF16))

    p_hi = p_sum.astype(BF16)
    p_lo = (p_sum - p_hi.astype(F32)).astype(BF16)
    ovl = ovl_ref[...]
    imp = _dot(ovl, p_hi) + _dot(ovl, p_lo)
    blk = lax.broadcasted_iota(jnp.int32, (n_blocks, TQ), 0)
    tq = qb * TQ + lax.broadcasted_iota(jnp.int32, (n_blocks, TQ), 1)
    cur = tq >> sel_shift
    forced = (blk == 0) | (blk == cur) | (blk == cur - 1)
    imp = jnp.where(forced, FORCE, imp)
    imp = jnp.where((blk << sel_shift) <= tq, imp, NEG)
    selb = jnp.full((n_blocks, TQ), NEG, F32)
    for _ in range(min(N_SELECT, n_blocks)):
        top = jnp.max(imp, axis=0, keepdims=True)
        first = jnp.min(jnp.where(imp == top, blk, n_blocks), axis=0, keepdims=True)
        pick = blk == first
        selb = jnp.where(pick, 0.0, selb)
        imp = jnp.where(pick, -jnp.inf, imp)
    for b in range(n_blocks):
        selb_ref[b] = jnp.broadcast_to(selb[b:b + 1, :], (SUBLANES, TQ))

    heads = range(heads_per_group)
    qs = [q_of(r) for r in heads]
    chunks = [slice(c, c + KEY_CHUNK) for c in range(0, TQ, KEY_CHUNK)]

    def key_rows(ref, kt):
        return ref[0, 0, pl.ds(pl.multiple_of(kt * TQ, TQ), TQ), :]

    def sel_bias(kt):
        tile = jnp.clip(kt - qb + 2, 0, 2)
        row0 = pl.multiple_of(lay["far"] + tile * TQ, TQ)
        masks = [jnp.tile(selb_ref[kt * blocks_per_tile + c.start // SEL_LEN], (KEY_CHUNK // SUBLANES, 1))
                 for c in chunks]
        return lambda r, n: tab_ref[r, pl.ds(row0 + chunks[n].start, KEY_CHUNK), :] + masks[n]

    def win_bias(it):
        row0 = pl.multiple_of(jnp.where(it == 0, lay["diag"], jnp.where(it == 1, lay["prev"], lay["win_far"])), TQ)
        return lambda r, n: tab_ref[r, pl.ds(row0 + chunks[n].start, KEY_CHUNK), :]

    def branch(which, n_steps, keys_of, value_of, bias_of, acc_ref):
        last = n_steps - 1
        s_ref = s_all_ref.at[which]
        p_ref = p_all_ref.at[which]

        def scores(i):
            k = keys_of(jnp.minimum(i, last))
            slot = i % 2
            for r in heads:
                s_ref[slot, r] = _dot(k, qs[r]) * scale

        def accumulate(i, alphas):
            v_t = value_of(jnp.maximum(i, 0))
            slot = (i + 2) % 2
            for r in heads:
                acc_ref[r] = alphas[r] * acc_ref[r] + _dot(v_t, p_ref[slot, r])

        for r in heads:
            acc_ref[r] = jnp.zeros((dk, TQ), F32)
            p_ref[1, r] = jnp.zeros((TQ, TQ), BF16)
        scores(0)

        def step(i, carry):
            ms, ls, alphas = carry
            accumulate(i - 1, alphas)
            bias = bias_of(i)
            slot = i % 2
            new_m, new_l, new_a = [], [], []
            for r in heads:
                m = ms[r]
                for n, c in enumerate(chunks):
                    s = s_ref[slot, r, c, :] + bias(r, n)
                    s_ref[slot, r, c, :] = s
                    m = jnp.maximum(m, jnp.max(s, axis=0, keepdims=True))
                alpha = jnp.exp(ms[r] - m)
                l = alpha * ls[r]
                for c in chunks:
                    p = jnp.exp(s_ref[slot, r, c, :] - m)
                    l = l + jnp.sum(p, axis=0, keepdims=True)
                    p_ref[slot, r, c, :] = p.astype(BF16)
                new_m.append(m)
                new_l.append(l)
                new_a.append(alpha)
            scores(i + 1)
            return tuple(new_m), tuple(new_l), tuple(new_a)

        init = (tuple(jnp.full((1, TQ), NEG, F32) for _ in heads), tuple(jnp.zeros((1, TQ), F32) for _ in heads),
                tuple(jnp.ones((1, TQ), F32) for _ in heads))
        _, ls, alphas = lax.fori_loop(0, n_steps, step, init)
        accumulate(last, alphas)
        return ls

    l_s = branch(0, qb + 1, lambda kt: key_rows(ks_ref, kt), lambda kt: vst_ref[0, kt], sel_bias, accs_ref)
    l_w = branch(1, jnp.minimum(qb, WINDOW // TQ) + 1, lambda it: key_rows(kw_ref, qb - it),
                 lambda it: vwt_ref[0, qb - it], win_bias, accw_ref)

    gates = gt_ref[0, 0]
    for r in heads:
        g_c = gates[3 * r:3 * r + 1, :]
        g_s = gates[3 * r + 1:3 * r + 2, :]
        g_w = gates[3 * r + 2:3 * r + 3, :]
        out_t = g_c * oc_ref[r] + (g_s / l_s[r]) * accs_ref[r] + (g_w / l_w[r]) * accw_ref[r]
        o_ref[0, :, r * dk:(r + 1) * dk] = out_t.T.astype(o_ref.dtype)


def _gate_rows_per_group():
    return -(-3 * GROUP_SIZE // SUBLANES) * SUBLANES


def _attention(proj_t, gates_t, kv, kc, vct, tables, batch, seq, heads, groups, dk, lay):
    assert WINDOW == 2 * TQ and TQ % SEL_LEN == 0 and seq % TQ == 0
    r = heads // groups
    n_tiles = seq // TQ
    slots = seq // CMP_STRIDE
    n_blocks = seq // SEL_LEN
    q_rows = heads * dk
    kv_rows = groups * dk
    gate_rows = _gate_rows_per_group()
    n0 = np.arange(slots)[None, :] * CMP_STRIDE
    s0 = np.arange(n_blocks)[:, None] * SEL_LEN
    ovl = jnp.asarray(((n0 < s0 + SEL_LEN) & (n0 + 2 * CMP_STRIDE > s0) & (n0 + 2 * CMP_STRIDE <= seq)), dtype=BF16)
    vs_block0 = q_rows // dk
    vw_block0 = (q_rows + kv_rows) // dk
    return pl.pallas_call(
        functools.partial(_attention_body, seq, lay),
        out_shape=jax.ShapeDtypeStruct((batch, seq, q_rows), BF16),
        grid=(batch, groups, n_tiles),
        in_specs=[
            pl.BlockSpec((1, 1, r * dk, TQ), lambda b, g, t: (b, t, g, 0)),
            pl.BlockSpec((1, 1, slots, dk), lambda b, g, t: (b, g, 0, 0)),
            pl.BlockSpec((1, 1, dk, slots), lambda b, g, t: (b, g, 0, 0)),
            pl.BlockSpec((1, 1, seq, dk), lambda b, g, t: (2, g, b, 0)),
            pl.BlockSpec((1, 1, seq, dk), lambda b, g, t: (3, g, b, 0)),
            pl.BlockSpec((1, n_tiles, dk, TQ), lambda b, g, t: (b, 0, vs_block0 + g, 0)),
            pl.BlockSpec((1, n_tiles, dk, TQ), lambda b, g, t: (b, 0, vw_block0 + g, 0)),
            pl.BlockSpec((1, 1, gate_rows, TQ), lambda b, g, t: (b, t, g, 0)),
            pl.BlockSpec((r, lay["rows"], TQ), lambda b, g, t: (g, 0, 0)),
            pl.BlockSpec((n_blocks, slots), lambda b, g, t: (0, 0)),
        ],
        out_specs=pl.BlockSpec((1, TQ, r * dk), lambda b, g, t: (b, t, g)),
        scratch_shapes=[pltpu.VMEM((r, dk, TQ), F32), pltpu.VMEM((r, dk, TQ), F32), pltpu.VMEM((r, dk, TQ), F32),
                        pltpu.VMEM((n_blocks, SUBLANES, TQ), F32),
                        pltpu.VMEM((2, 2, r, TQ, TQ), F32), pltpu.VMEM((2, 2, r, TQ, TQ), BF16)],
        compiler_params=_params("parallel", "parallel", "arbitrary"),
        name="nsa_attention",
    )(proj_t, kc, vct, kv, kv, proj_t, proj_t, gates_t, tables, ovl)


def _nsa_out_body(final, o_ref, z_ref, x_ref, w_ref, gn_ref, *rest):
    if final:
        xo_ref, lhs_ref = rest
        ho_ref = None
    else:
        xo_ref, ho_ref, lhs_ref = rest
    tm = o_ref.shape[0]

    def chunk(c, carry):
        r0 = pl.multiple_of(c * BF16_ROWS, BF16_ROWS)
        z = z_ref[pl.ds(r0, BF16_ROWS), :].astype(F32)
        o = o_ref[pl.ds(r0, BF16_ROWS), :].astype(F32)
        lhs_ref[pl.ds(r0, BF16_ROWS), :] = (o * (z * _sigmoid(z))).astype(lhs_ref.dtype)
        return carry

    lax.fori_loop(0, tm // BF16_ROWS, chunk, 0)
    _out_proj_tail(lhs_ref, w_ref, x_ref, gn_ref, xo_ref, ho_ref)


def _nsa_out(o, z, x, w_out, g_next, final, tm=256):
    m, e = o.shape
    d = w_out.shape[1]
    out_shape, out_specs = _out_proj_outputs(m, d, tm, final)
    return pl.pallas_call(
        functools.partial(_nsa_out_body, final),
        out_shape=out_shape,
        grid=(m // tm,),
        in_specs=[pl.BlockSpec((tm, e), lambda i: (i, 0)),
                  pl.BlockSpec((tm, e), lambda i: (i, 0)),
                  pl.BlockSpec((tm, d), lambda i: (i, 0)),
                  pl.BlockSpec((e, d), lambda i: (0, 0)),
                  pl.BlockSpec((1, d), lambda i: (0, 0))],
        out_specs=out_specs,
        scratch_shapes=[pltpu.VMEM((tm, e), BF16)],
        compiler_params=_params("parallel"),
        name="nsa_out",
    )(o, z, x, w_out, g_next.reshape(1, d))


def _conformer_layer(x, h, seq, w_in, dw_w, dw_b, ln_g, ln_b, w_out, g_next, final):
    v, sz = _conformer_in(h, w_in.astype(BF16))
    return _conformer_out(v, sz, x, dw_w, dw_b, ln_g, ln_b, w_out.astype(BF16), g_next, seq, final)


def _nsa_layer(x, h, batch, seq, heads, tables, lay, w_in, cmp_pos, ck_w1, ck_w2, cv_w1, cv_w2, w_out, g_next, final):
    d = x.shape[1]
    dk = d // heads
    groups = heads // GROUP_SIZE
    q_w = heads * dk
    kv_w = groups * dk
    gate_w = 3 * heads
    assert w_in.shape[1] == 2 * q_w + 6 * kv_w + gate_w
    cuts = np.cumsum([0, q_w] + [kv_w] * 6 + [gate_w, q_w])
    col = lambda k: w_in[:, cuts[k]:cuts[k + 1]]
    w_q, w_kc, w_vc, w_ks, w_vs, w_kw, w_vw, w_g, w_z = (col(k) for k in range(9))
    w_kv = jnp.concatenate([w_kc, w_vc, w_ks, w_kw], axis=1).astype(BF16)
    w_t = jnp.concatenate([w_q, w_vs, w_vw], axis=1).T.astype(BF16)
    per_group = 3 * GROUP_SIZE
    gate_rows = _gate_rows_per_group()
    w_g =jnp.pad(w_g.reshape(d, groups, per_group), ((0, 0), (0, 0), (0, gate_rows - per_group)))
    w_g_t = jnp.pad(w_g.reshape(d, groups * gate_rows).T, ((0, -groups * gate_rows % 128), (0, 0))).astype(BF16)

    kv = _kv_proj(h, w_kv, groups, dk)
    z = _matmul(h, w_z.astype(BF16))
    proj_t = _proj_t(h, w_t, batch, seq, BF16, False, "nsa_in_t")
    gates_t = _proj_t(h, w_g_t, batch, seq, F32, True, "nsa_in_gates")
    kc, vct = _compress(kv, cmp_pos, ck_w1.astype(BF16), ck_w2.astype(BF16), cv_w1.astype(BF16),
                        cv_w2.T.astype(BF16), batch, seq)
    o = _attention(proj_t, gates_t, kv, kc, vct, tables, batch, seq, heads, groups, dk, lay)
    return _nsa_out(o.reshape(batch * seq, q_w), z, x, w_out.astype(BF16), g_next, final)


def kernel(x, rel_bias, l0_norm, l0_w_in, l0_dw_w, l0_dw_b, l0_ln_g, l0_ln_b, l0_w_out, l1_norm, l1_w_in, l1_cmp_pos, l1_ck_w1, l1_ck_w2, l1_cv_w1, l1_cv_w2, l1_w_out, l2_norm, l2_w_in, l2_dw_w, l2_dw_b, l2_ln_g, l2_ln_b, l2_w_out, l3_norm, l3_w_in, l3_cmp_pos, l3_ck_w1, l3_ck_w2, l3_cv_w1, l3_cv_w2, l3_w_out, final_norm):
    batch, seq, d = x.shape
    heads = rel_bias.shape[1]
    lay = _table_layout(seq)
    tables = _bias_tables(rel_bias, seq, l1_cmp_pos.shape[0])
    x2 = x.reshape(batch * seq, d)
    h = _rmsnorm(x2, l0_norm)
    x2, h = _conformer_layer(x2, h, seq, l0_w_in, l0_dw_w, l0_dw_b, l0_ln_g, l0_ln_b, l0_w_out, l1_norm, False)
    x2, h = _nsa_layer(x2, h, batch, seq, heads, tables, lay, l1_w_in, l1_cmp_pos, l1_ck_w1, l1_ck_w2, l1_cv_w1,
                       l1_cv_w2, l1_w_out, l2_norm, False)
    x2, h = _conformer_layer(x2, h, seq, l2_w_in, l2_dw_w, l2_dw_b, l2_ln_g, l2_ln_b, l2_w_out, l3_norm, False)
    y = _nsa_layer(x2, h, batch, seq, heads, tables, lay, l3_w_in, l3_cmp_pos, l3_ck_w1, l3_ck_w2, l3_cv_w1,
                   l3_cv_w2, l3_w_out, final_norm, True)
    return y.reshape(batch, seq, d)
```

```python
import functools
import math

import jax
import jax.numpy as jnp
import numpy as np
from jax import lax
from jax.experimental import pallas as pl
from jax.experimental.pallas import tpu as pltpu

F32 = jnp.float32
BF16 = jnp.bfloat16

GROUP_SIZE = 4
CMP_STRIDE = 16
SEL_LEN = 64
N_SELECT = 8
WINDOW = 512
MAX_DISTANCE = 128
EPS = 1e-6
LOG2E = math.log2(math.e)
NEG = -1e30
FORCE = 1e6

TQ = 256
PROJ_ROWS = 1024
KEY_CHUNK = 64
V7X_VMEM_LIMIT_BYTES = 56 * 1024 * 1024
SUBLANES = 8
NORM_ROWS = 32


def _params(*semantics):
    return pltpu.CompilerParams(dimension_semantics=semantics, vmem_limit_bytes=V7X_VMEM_LIMIT_BYTES)


def _sigmoid(x):
    return 1.0 / (1.0 + jnp.exp(-x))


def _dot(a, b):
    return jnp.dot(a, b, preferred_element_type=F32)


def _dot_nt(a, b):
    return lax.dot_general(a, b, (((1,), (1,)), ((), ())), preferred_element_type=F32)


def _rmsnorm_body(x_ref, g_ref, o_ref):
    x = x_ref[...]
    ms = jnp.mean(x * x, axis=-1, keepdims=True)
    o_ref[...] = (x * lax.rsqrt(ms + EPS) * g_ref[...]).astype(o_ref.dtype)


def _rmsnorm(x2d, gain, tm=512):
    m, d = x2d.shape
    return pl.pallas_call(
        _rmsnorm_body,
        out_shape=jax.ShapeDtypeStruct((m, d), BF16),
        grid=(m // tm,),
        in_specs=[pl.BlockSpec((tm, d), lambda i: (i, 0)), pl.BlockSpec((1, d), lambda i: (0, 0))],
        out_specs=pl.BlockSpec((tm, d), lambda i: (i, 0)),
        compiler_params=_params("parallel"),
        name="rmsnorm",
    )(x2d, gain.reshape(1, d))


def _out_proj_tail(lhs_ref, w_ref, x_ref, gn_ref, xo_ref, ho_ref):
    tm = xo_ref.shape[0]
    xo_ref[...] = x_ref[...] + _dot(lhs_ref[...], w_ref[...])

    def chunk(c, carry):
        r0 = pl.multiple_of(c * NORM_ROWS, NORM_ROWS)
        xn = xo_ref[pl.ds(r0, NORM_ROWS), :]
        ms = jnp.mean(xn * xn, axis=-1, keepdims=True)
        hn = xn * lax.rsqrt(ms + EPS) * gn_ref[...]
        if ho_ref is None:
            xo_ref[pl.ds(r0, NORM_ROWS), :] = hn
        else:
            ho_ref[pl.ds(r0, NORM_ROWS), :] = hn.astype(ho_ref.dtype)
        return carry

    lax.fori_loop(0, tm // NORM_ROWS, chunk, 0)


def _out_proj_outputs(m, d, tm, final):
    x_spec = pl.BlockSpec((tm, d), lambda i: (i, 0))
    if final:
        return jax.ShapeDtypeStruct((m, d), F32), x_spec
    return ((jax.ShapeDtypeStruct((m, d), F32), jax.ShapeDtypeStruct((m, d), BF16)), (x_spec, x_spec))


def _conformer_in_body(h_ref, wa_ref, wb_ref, wz_ref, v_ref, sz_ref):
    h = h_ref[...]
    a = _dot(h, wa_ref[...])
    b = _dot(h, wb_ref[...])
    z = _dot(h, wz_ref[...])
    v_ref[...] = (a * _sigmoid(b)).astype(v_ref.dtype)
    sz_ref[...] = (z * _sigmoid(z)).astype(sz_ref.dtype)


def _conformer_in(h, w_in, tm=PROJ_ROWS, tn=512):
    m, d = h.shape
    e = w_in.shape[1] // 3
    nj = e // tn
    h_spec = pl.BlockSpec((tm, d), lambda i, j: (i, 0))
    o_spec = pl.BlockSpec((tm, tn), lambda i, j: (i, j))
    return pl.pallas_call(
        _conformer_in_body,
        out_shape=(jax.ShapeDtypeStruct((m, e), F32), jax.ShapeDtypeStruct((m, e), BF16)),
        grid=(m // tm, nj),
        in_specs=[h_spec,
                  pl.BlockSpec((d, tn), lambda i, j: (0, j)),
                  pl.BlockSpec((d, tn), lambda i, j: (0, j + nj)),
                  pl.BlockSpec((d, tn), lambda i, j: (0, j + 2 * nj))],
        out_specs=(o_spec, o_spec),
        compiler_params=_params("parallel", "parallel"),
        name="conformer_in",
    )(h, w_in, w_in, w_in)


CONV_HALO = 32
CONV_ROWS = 32
CONV_COLS = 512


def _conformer_out_body(tiles_per_seq, final, v_ref, halo_ref, sz_ref, x_ref, dww_ref, dwb_ref, lng_ref, lnb_ref,
                        w_ref, gn_ref, *rest):
    if final:
        xo_ref, vb_ref, sh_ref, y_ref, lhs_ref = rest
        ho_ref = None
    else:
        xo_ref, ho_ref, vb_ref, sh_ref, y_ref, lhs_ref = rest
    tm, e = v_ref.shape
    width = dww_ref.shape[0]
    first = (pl.program_id(0) % tiles_per_seq) == 0
    vb_ref[0:CONV_HALO, :] = jnp.where(first, 0.0, halo_ref[...])
    vb_ref[CONV_HALO:CONV_HALO + tm, :] = v_ref[...]
    vb_ref[CONV_HALO + tm:CONV_HALO + tm + SUBLANES, :] = jnp.zeros((SUBLANES, e), F32)

    for cb in range(e // CONV_COLS):
        cs = slice(cb * CONV_COLS, (cb + 1) * CONV_COLS)

        def shift_chunk(c, carry, cs=cs):
            r0 = pl.multiple_of(c * CONV_ROWS, CONV_ROWS)
            window = vb_ref[pl.ds(r0, CONV_ROWS + SUBLANES), cs]
            for r in range(1, SUBLANES):
                sh_ref[r - 1, pl.ds(r0, CONV_ROWS), :] = window[r:r + CONV_ROWS, :]
            return carry

        lax.fori_loop(0, (CONV_HALO + tm) // CONV_ROWS, shift_chunk, 0)

        def conv_chunk(c, carry, cs=cs):
            r0 = pl.multiple_of(c * CONV_ROWS, CONV_ROWS)
            acc = jnp.broadcast_to(dwb_ref[:, cs], (CONV_ROWS, CONV_COLS))
            for k in range(width):
                groups, r = divmod(CONV_HALO - (width - 1) + k, SUBLANES)
                rows = pl.ds(pl.multiple_of(r0 + groups * SUBLANES, SUBLANES), CONV_ROWS)
                src = vb_ref[rows, cs] if r == 0 else sh_ref[r - 1, rows, :]
                acc = acc + src * jnp.tile(dww_ref[k, :, cs], (CONV_ROWS // SUBLANES, 1))
            y_ref[pl.ds(r0, CONV_ROWS), cs] = acc
            return carry

        lax.fori_loop(0, tm // CONV_ROWS, conv_chunk, 0)

    def ln_chunk(c, carry):
        r0 = pl.multiple_of(c * NORM_ROWS, NORM_ROWS)
        y = y_ref[pl.ds(r0, NORM_ROWS), :]
        mu = jnp.mean(y, axis=-1, keepdims=True)
        dlt = y - mu
        var = jnp.mean(dlt * dlt, axis=-1, keepdims=True)
        yn = dlt * lax.rsqrt(var + EPS) * lng_ref[...] + lnb_ref[...]
        act = yn * _sigmoid(yn) * sz_ref[pl.ds(r0, NORM_ROWS), :].astype(F32)
        lhs_ref[pl.ds(r0, NORM_ROWS), :] = act.astype(lhs_ref.dtype)
        return carry

    lax.fori_loop(0, tm // NORM_ROWS, ln_chunk, 0)
    _out_proj_tail(lhs_ref, w_ref, x_ref, gn_ref, xo_ref, ho_ref)


def _conformer_out(v, sz, x, dw_w, dw_b, ln_g, ln_b, w_out, g_next, seq, final, tm=256):
    m, e = v.shape
    d = w_out.shape[1]
    width = dw_w.shape[0]
    assert width - 1 <= CONV_HALO and seq % tm == 0 and tm % CONV_HALO == 0
    halo_blocks = tm // CONV_HALO
    row = lambda a: a.reshape(1, -1)
    out_shape, out_specs = _out_proj_outputs(m, d, tm, final)
    return pl.pallas_call(
        functools.partial(_conformer_out_body, seq // tm, final),
        out_shape=out_shape,
        grid=(m // tm,),
        in_specs=[pl.BlockSpec((tm, e), lambda i: (i, 0)),
                  pl.BlockSpec((CONV_HALO, e), lambda i: (jnp.maximum(i * halo_blocks - 1, 0), 0)),
                  pl.BlockSpec((tm, e), lambda i: (i, 0)),
                  pl.BlockSpec((tm, d), lambda i: (i, 0)),
                  pl.BlockSpec((width, SUBLANES, e), lambda i: (0, 0, 0)),
                  pl.BlockSpec((1, e), lambda i: (0, 0)),
                  pl.BlockSpec((1, e), lambda i: (0, 0)),
                  pl.BlockSpec((1, e), lambda i: (0, 0)),
                  pl.BlockSpec((e, d), lambda i: (0, 0)),
                  pl.BlockSpec((1, d), lambda i: (0, 0))],
        out_specs=out_specs,
        scratch_shapes=[pltpu.VMEM((CONV_HALO + tm + SUBLANES, e), F32),
                        pltpu.VMEM((SUBLANES - 1, CONV_HALO + tm, CONV_COLS), F32),
                        pltpu.VMEM((tm, e), F32), pltpu.VMEM((tm, e), BF16)],
        compiler_params=_params("parallel"),
        name="conformer_out",
    )(v, v, sz, x, jnp.broadcast_to(dw_w[:, None, :], (width, SUBLANES, e)), row(dw_b), row(ln_g), row(ln_b), w_out, row(g_next))


def _matmul_body(h_ref, w_ref, o_ref):
    o_ref[...] = _dot(h_ref[...], w_ref[...]).astype(o_ref.dtype)


def _matmul(h, w, tm=PROJ_ROWS, tn=512):
    m, d = h.shape
    n = w.shape[1]
    return pl.pallas_call(
        _matmul_body,
        out_shape=jax.ShapeDtypeStruct((m, n), BF16),
        grid=(m // tm, n // tn),
        in_specs=[pl.BlockSpec((tm, d), lambda i, j: (i, 0)), pl.BlockSpec((d, tn), lambda i, j: (0, j))],
        out_specs=pl.BlockSpec((tm, tn), lambda i, j: (i, j)),
        compiler_params=_params("parallel", "parallel"),
        name="nsa_in_z",
    )(h, w)


def _kv_proj_body(h_ref, w_ref, o_ref):
    groups, _, dk = o_ref.shape[1:]
    res = _dot(h_ref[...], w_ref[...]).astype(o_ref.dtype)
    for g in range(groups):
        o_ref[0, g] = res[:, g * dk:(g + 1) * dk]


def _kv_proj(h, w, groups, dk, tm=PROJ_ROWS):
    m, d = h.shape
    kinds = w.shape[1] // (groups * dk)
    return pl.pallas_call(
        _kv_proj_body,
        out_shape=jax.ShapeDtypeStruct((kinds, groups, m, dk), BF16),
        grid=(m // tm, kinds),
        in_specs=[pl.BlockSpec((tm, d), lambda i, j: (i, 0)), pl.BlockSpec((d, groups * dk), lambda i, j: (0, j))],
        out_specs=pl.BlockSpec((1, groups, tm, dk), lambda i, j: (j, 0, i, 0)),
        compiler_params=_params("parallel", "parallel"),
        name="nsa_in_kv",
    )(h, w)


def _proj_q_t_body(scale, w_ref, h_ref, o_ref):
    dk = o_ref.shape[3]
    res = (_dot_nt(w_ref[...], h_ref[...]) * scale).astype(o_ref.dtype)
    for c in range(o_ref.shape[1]):
        for r in range(w_ref.shape[0] // dk):
            o_ref[0, c, 0, :, r * TQ:(r + 1) * TQ] = res[r * dk:(r + 1) * dk, c * TQ:(c + 1) * TQ]


def _proj_q_t(h, w_t, batch, seq, groups, dk, scale, tm=PROJ_ROWS):
    m, d = h.shape
    rows = w_t.shape[0] // groups
    tiles = seq // tm
    return pl.pallas_call(
        functools.partial(_proj_q_t_body, scale),
        out_shape=jax.ShapeDtypeStruct((batch, seq // TQ, groups, dk, rows // dk * TQ), BF16),
        grid=(m // tm, groups),
        in_specs=[pl.BlockSpec((rows, d), lambda i, g: (g, 0)), pl.BlockSpec((tm, d), lambda i, g: (i, 0))],
        out_specs=pl.BlockSpec((1, tm // TQ, 1, dk, rows // dk * TQ), lambda i, g: (i // tiles, i % tiles, g, 0, 0)),
        compiler_params=_params("parallel", "parallel"),
        name="nsa_in_q",
    )(w_t, h)


def _proj_t_body(sigmoid, w_ref, h_ref, o_ref):
    res = _dot_nt(w_ref[...], h_ref[...])
    if sigmoid:
        res = _sigmoid(res)
    res = res.astype(o_ref.dtype)
    for c in range(o_ref.shape[1]):
        o_ref[0, c] = res[:, c * TQ:(c + 1) * TQ]


def _proj_t(h, w_t, batch, seq, out_dtype, sigmoid, name, tm=PROJ_ROWS):
    m, d = h.shape
    rows = w_t.shape[0]
    tn = next(t for t in (512, 256, 128) if rows % t == 0)
    tiles = seq // tm
    return pl.pallas_call(
        functools.partial(_proj_t_body, sigmoid),
        out_shape=jax.ShapeDtypeStruct((batch, seq // TQ, rows, TQ), out_dtype),
        grid=(m // tm, rows // tn),
        in_specs=[pl.BlockSpec((tn, d), lambda i, j: (j, 0)), pl.BlockSpec((tm, d), lambda i, j: (i, 0))],
        out_specs=pl.BlockSpec((1, tm // TQ, tn, TQ), lambda i, j: (i // tiles, i % tiles, j, 0)),
        compiler_params=_params("parallel", "parallel"),
        name=name,
    )(w_t, h)


def _compress_body(xk_ref, xv_ref, pos_ref, w1k_ref, w2k_ref, w1v_ref, w2v_ref, kc_ref, vct_ref):
    slots = xk_ref.shape[2]

    def hidden(x_ref, w1_ref):
        x = x_ref[0, 0].astype(F32)
        first = _dot((x + pos_ref[0]).astype(BF16), w1_ref[0])
        second = _dot((x + pos_ref[1]).astype(BF16), w1_ref[1])
        pre = first + pltpu.roll(second, slots - 1, 0)
        return (pre * _sigmoid(pre)).astype(BF16)

    kc_ref[0, 0] = _dot(hidden(xk_ref, w1k_ref), w2k_ref[...]).astype(kc_ref.dtype)
    vct_ref[0, 0] = _dot_nt(w2v_ref[...], hidden(xv_ref, w1v_ref)).astype(vct_ref.dtype)


def _compress(kv, pos, w1k, w2k, w1v, w2v, batch, seq):
    kinds, groups, m, dk = kv.shape
    slots = seq // CMP_STRIDE
    assert pos.shape[0] == 2 * CMP_STRIDE
    x = kv.reshape(kinds, groups, m // CMP_STRIDE, CMP_STRIDE * dk)
    pos8 = pos.reshape(2, 1, CMP_STRIDE * dk)
    w1k = w1k.reshape(2, CMP_STRIDE * dk, dk)
    w1v = w1v.reshape(2, CMP_STRIDE * dk, dk)
    full = lambda a: pl.BlockSpec(a.shape, lambda b, g: (0,) * a.ndim)
    return pl.pallas_call(
        _compress_body,
        out_shape=(jax.ShapeDtypeStruct((batch, groups, slots, dk), BF16),
                   jax.ShapeDtypeStruct((batch, groups, dk, slots), BF16)),
        grid=(batch, groups),
        in_specs=[pl.BlockSpec((1, 1, slots, CMP_STRIDE * dk), lambda b, g: (0, g, b, 0)),
                  pl.BlockSpec((1, 1, slots, CMP_STRIDE * dk), lambda b, g: (1, g, b, 0)),
                  full(pos8), full(w1k), full(w2k), full(w1v), full(w2v)],
        out_specs=(pl.BlockSpec((1, 1, slots, dk), lambda b, g: (b, g, 0, 0)),
                   pl.BlockSpec((1, 1, dk, slots), lambda b, g: (b, g, 0, 0))),
        compiler_params=_params("parallel", "parallel"),
        name="nsa_compress",
    )(x, x, pos8, w1k, w2k, w1v, w2v)


def _t5_bucket_np(dist, num_buckets):
    dist = np.maximum(dist, 0)
    max_exact = num_buckets // 2
    d = np.maximum(dist, max_exact).astype(np.float32)
    ratio = np.log(d / np.float32(max_exact)) / np.float32(math.log(MAX_DISTANCE / max_exact))
    large = max_exact + (ratio * np.float32(num_buckets - max_exact)).astype(np.int32)
    large = np.minimum(large, num_buckets - 1)
    return np.where(dist < max_exact, dist, large).astype(np.int32)


def _table_layout(seq):
    slots = seq // CMP_STRIDE
    cmp_rows = slots + (seq // TQ - 1) * (TQ // CMP_STRIDE)
    cmp_rows = -(-cmp_rows // 64) * 64
    return dict(win_far=0, far=TQ, prev=2 * TQ, diag=3 * TQ, cmp=4 * TQ, rows=4 * TQ + cmp_rows, cmp_rows=cmp_rows)


def _bucket_index_tables(seq, num_buckets, cmp_len):
    lay = _table_layout(seq)
    j = np.arange(TQ)[:, None]
    i = np.arange(TQ)[None, :]
    far_bucket = _t5_bucket_np(np.full((TQ, TQ), 2 * TQ), num_buckets)
    assert TQ >= MAX_DISTANCE
    win_far = np.where(i < j, far_bucket, -1)
    far = far_bucket
    prev = _t5_bucket_np(i - j + TQ, num_buckets)
    diag = np.where(i >= j, _t5_bucket_np(i - j, num_buckets), -1)
    rho = np.arange(lay["cmp_rows"])[:, None]
    rel_slot = rho - (seq // TQ - 1) * (TQ // CMP_STRIDE)
    dist_c = i - CMP_STRIDE * rel_slot - (cmp_len - 1)
    cmp = np.where(dist_c >= 0, _t5_bucket_np(dist_c, num_buckets), -1)
    return np.concatenate([win_far, far, prev, diag, cmp], axis=0).astype(np.int32)


TABLE_CHUNK = 64


def _bias_table_body(rb_ref, idx_ref, o_ref):
    head = pl.program_id(0)
    num_buckets = rb_ref.shape[0]

    def chunk(c, carry):
        r0 = pl.multiple_of(c * TABLE_CHUNK, TABLE_CHUNK)
        idx = idx_ref[pl.ds(r0, TABLE_CHUNK), :]
        acc = jnp.full(idx.shape, NEG, F32)
        for k in range(num_buckets):
            acc = jnp.where(idx == k, rb_ref[k, head] * LOG2E, acc)
        o_ref[0, pl.ds(r0, TABLE_CHUNK), :] = acc
        return carry

    lax.fori_loop(0, idx_ref.shape[0] // TABLE_CHUNK, chunk, 0)


def _bias_tables(rel_bias, seq, cmp_len):
    num_buckets, heads = rel_bias.shape
    idx = jnp.asarray(_bucket_index_tables(seq, num_buckets, cmp_len))
    rows = idx.shape[0]
    return pl.pallas_call(
        _bias_table_body,
        out_shape=jax.ShapeDtypeStruct((heads, rows, TQ), F32),
        grid=(heads,),
        in_specs=[pl.BlockSpec(memory_space=pltpu.SMEM), pl.BlockSpec((rows, TQ), lambda h: (0, 0))],
        out_specs=pl.BlockSpec((1, rows, TQ), lambda h: (h, 0, 0)),
        compiler_params=_params("parallel"),
        name="t5_bias_tables",
    )(rel_bias, idx)


def _attention_body(seq, lay, qt_ref, kc_ref, vct_ref, ks_ref, kw_ref, vst_ref, vwt_ref, gt_ref, tab_ref, ovl_ref,
                    o_ref, oc_ref, accs_ref, accw_ref, selb_ref, s_all_ref, p_all_ref):
    qb = pl.program_id(2)
    heads_per_group = tab_ref.shape[0]
    dk = kc_ref.shape[3]
    slots = kc_ref.shape[2]
    n_blocks = ovl_ref.shape[0]
    n_tiles = seq // TQ
    slots_per_tile = TQ // CMP_STRIDE
    blocks_per_tile = TQ // SEL_LEN
    sel_shift = SEL_LEN.bit_length() - 1

    heads = range(heads_per_group)
    lanes = [slice(r * TQ, (r + 1) * TQ) for r in heads]
    q_all = qt_ref[0, 0, 0]

    cmp_row0 = lay["cmp"] + pl.multiple_of((n_tiles - 1 - qb) * slots_per_tile, slots_per_tile)
    s_cmp = _dot(kc_ref[0, 0], q_all)
    p_sum = jnp.zeros((slots, TQ), F32)
    p_cmp = []
    for r in heads:
        s = s_cmp[:, lanes[r]] + tab_ref[r, pl.ds(cmp_row0, slots), :]
        valid = s > 0.5 * NEG
        m = jnp.max(s, axis=0, keepdims=True)
        p = jnp.where(valid, jnp.exp2(s - m), 0.0)
        l = jnp.sum(p, axis=0, keepdims=True)
        p = p * jnp.where(l > 0.0, 1.0 / l, 0.0)
        p_sum = p_sum + p
        p_cmp.append(p.astype(BF16))
    oc_ref[...] = _dot(vct_ref[0, 0], jnp.concatenate(p_cmp, axis=1))

    p_hi = p_sum.astype(BF16)
    p_lo = (p_sum - p_hi.astype(F32)).astype(BF16)
    ovl = ovl_ref[...]
    imp = _dot(ovl, p_hi) + _dot(ovl, p_lo)
    blk = lax.broadcasted_iota(jnp.int32, (n_blocks, TQ), 0)
    tq = qb * TQ + lax.broadcasted_iota(jnp.int32, (n_blocks, TQ), 1)
    cur = tq >> sel_shift
    forced = (blk == 0) | (blk == cur) | (blk == cur - 1)
    imp = jnp.where(forced, FORCE, imp)
    imp = jnp.where((blk << sel_shift) <= tq, imp, NEG)
    selb = jnp.full((n_blocks, TQ), NEG, F32)
    for _ in range(min(N_SELECT, n_blocks)):
        top = jnp.max(imp, axis=0, keepdims=True)
        first = jnp.min(jnp.where(imp == top, blk, n_blocks), axis=0, keepdims=True)
        pick = blk == first
        selb = jnp.where(pick, 0.0, selb)
        imp = jnp.where(pick, -jnp.inf, imp)
    for b in range(n_blocks):
        selb_ref[b] = jnp.broadcast_to(selb[b:b + 1, :], (SUBLANES, TQ))

    chunks = [slice(c, c + KEY_CHUNK) for c in range(0, TQ, KEY_CHUNK)]

    def key_rows(ref, kt):
        return ref[0, 0, pl.ds(pl.multiple_of(kt * TQ, TQ), TQ), :]

    def sel_bias(kt):
        tile = jnp.clip(kt - qb + 2, 0, 2)
        row0 = pl.multiple_of(lay["far"] + tile * TQ, TQ)
        masks = [jnp.tile(selb_ref[kt * blocks_per_tile + c.start // SEL_LEN], (KEY_CHUNK // SUBLANES, 1))
                 for c in chunks]
        return lambda r, n: tab_ref[r, pl.ds(row0 + chunks[n].start, KEY_CHUNK), :] + masks[n]

    def win_bias(it):
        row0 = pl.multiple_of(jnp.where(it == 0, lay["diag"], jnp.where(it == 1, lay["prev"], lay["win_far"])), TQ)
        return lambda r, n: tab_ref[r, pl.ds(row0 + chunks[n].start, KEY_CHUNK), :]

    def branch(which, n_steps, keys_of, value_of, bias_of, acc_ref):
        last = n_steps - 1
        s_ref = s_all_ref.at[which]
        p_ref = p_all_ref.at[which]

        def scores(i):
            s_ref[i % 2] = _dot(keys_of(jnp.minimum(i, last)), q_all)

        def accumulate(i, alphas):
            alpha = jnp.concatenate(alphas, axis=1)
            acc_ref[...] = alpha * acc_ref[...] + _dot(value_of(jnp.maximum(i, 0)), p_ref[(i + 2) % 2])

        acc_ref[...] = jnp.zeros(acc_ref.shape, F32)
        p_ref[1] = jnp.zeros(p_ref.shape[1:], BF16)
        scores(0)

        def step(i, carry):
            ms, ls, alphas = carry
            accumulate(i - 1, alphas)
            bias = bias_of(i)
            slot = i % 2
            new_m, new_l, new_a = [], [], []
            for r in heads:
                m = ms[r]
                for n, c in enumerate(chunks):
                    s = s_ref[slot, c, lanes[r]] + bias(r, n)
                    s_ref[slot, c, lanes[r]] = s
                    m = jnp.maximum(m, jnp.max(s, axis=0, keepdims=True))
                alpha = jnp.exp2(ms[r] - m)
                l = alpha * ls[r]
                for c in chunks:
                    p = jnp.exp2(s_ref[slot, c, lanes[r]] - m)
                    l = l + jnp.sum(p, axis=0, keepdims=True)
                    p_ref[slot, c, lanes[r]] = p.astype(BF16)
                new_m.append(m)
                new_l.append(l)
                new_a.append(alpha)
            scores(i + 1)
            return tuple(new_m), tuple(new_l), tuple(new_a)

        init = (tuple(jnp.full((1, TQ), NEG, F32) for _ in heads), tuple(jnp.zeros((1, TQ), F32) for _ in heads),
                tuple(jnp.ones((1, TQ), F32) for _ in heads))
        _, ls, alphas = lax.fori_loop(0, n_steps, step, init)
        accumulate(last, alphas)
        return ls

    l_s = branch(0, qb + 1, lambda kt: key_rows(ks_ref, kt), lambda kt: vst_ref[0, kt], sel_bias, accs_ref)
    l_w = branch(1, jnp.minimum(qb, WINDOW // TQ) + 1, lambda it: key_rows(kw_ref, qb - it),
                 lambda it: vwt_ref[0, qb - it], win_bias, accw_ref)

    gates = gt_ref[0, 0]
    for r in heads:
        g_c = gates[3 * r:3 * r + 1, :]
        g_s = gates[3 * r + 1:3 * r + 2, :]
        g_w = gates[3 * r + 2:3 * r + 3, :]
        out_t = (g_c * oc_ref[:, lanes[r]] + (g_s / l_s[r]) * accs_ref[:, lanes[r]]
                 + (g_w / l_w[r]) * accw_ref[:, lanes[r]])
        o_ref[0, :, r * dk:(r + 1) * dk] = out_t.T.astype(o_ref.dtype)


def _gate_rows_per_group():
    return -(-3 * GROUP_SIZE // SUBLANES) * SUBLANES


def _attention(q_t, v_t, gates_t, kv, kc, vct, tables, batch, seq, heads, groups, dk, lay):
    assert WINDOW == 2 * TQ and SEL_LEN % KEY_CHUNK == 0 and seq % TQ == 0
    r = heads // groups
    n_tiles = seq // TQ
    slots = seq // CMP_STRIDE
    n_blocks = seq // SEL_LEN
    gate_rows = _gate_rows_per_group()
    n0 = np.arange(slots)[None, :] * CMP_STRIDE
    s0 = np.arange(n_blocks)[:, None] * SEL_LEN
    ovl = jnp.asarray(((n0 < s0 + SEL_LEN) & (n0 + 2 * CMP_STRIDE > s0) & (n0 + 2 * CMP_STRIDE <= seq)), dtype=BF16)
    return pl.pallas_call(
        functools.partial(_attention_body, seq, lay),
        out_shape=jax.ShapeDtypeStruct((batch, seq, heads * dk), BF16),
        grid=(batch, groups, n_tiles),
        in_specs=[
            pl.BlockSpec((1, 1, 1, dk, r * TQ), lambda b, g, t: (b, t, g, 0, 0)),
            pl.BlockSpec((1, 1, slots, dk), lambda b, g, t: (b, g, 0, 0)),
            pl.BlockSpec((1, 1, dk, slots), lambda b, g, t: (b, g, 0, 0)),
            pl.BlockSpec((1, 1, seq, dk), lambda b, g, t: (2, g, b, 0)),
            pl.BlockSpec((1, 1, seq, dk), lambda b, g, t: (3, g, b, 0)),
            pl.BlockSpec((1, n_tiles, dk, TQ), lambda b, g, t: (b, 0, g, 0)),
            pl.BlockSpec((1, n_tiles, dk, TQ), lambda b, g, t: (b, 0, groups + g, 0)),
            pl.BlockSpec((1, 1, gate_rows, TQ), lambda b, g, t: (b, t, g, 0)),
            pl.BlockSpec((r, lay["rows"], TQ), lambda b, g, t: (g, 0, 0)),
            pl.BlockSpec((n_blocks, slots), lambda b, g, t: (0, 0)),
        ],
        out_specs=pl.BlockSpec((1, TQ, r * dk), lambda b, g, t: (b, t, g)),
        scratch_shapes=[pltpu.VMEM((dk, r * TQ), F32), pltpu.VMEM((dk, r * TQ), F32), pltpu.VMEM((dk, r * TQ), F32),
                        pltpu.VMEM((n_blocks, SUBLANES, TQ), F32),
                        pltpu.VMEM((2, 2, TQ, r * TQ), F32), pltpu.VMEM((2, 2, TQ, r * TQ), BF16)],
        compiler_params=_params("parallel", "parallel", "arbitrary"),
        name="nsa_attention",
    )(q_t, kc, vct, kv, kv, v_t, v_t, gates_t, tables, ovl)


def _nsa_out_body(final, o_ref, z_ref, x_ref, w_ref, gn_ref, *rest):
    if final:
        xo_ref, lhs_ref = rest
        ho_ref = None
    else:
        xo_ref, ho_ref, lhs_ref = rest
    tm = o_ref.shape[0]

    def chunk(c, carry):
        r0 = pl.multiple_of(c * NORM_ROWS, NORM_ROWS)
        z = z_ref[pl.ds(r0, NORM_ROWS), :].astype(F32)
        o = o_ref[pl.ds(r0, NORM_ROWS), :].astype(F32)
        lhs_ref[pl.ds(r0, NORM_ROWS), :] = (o * (z * _sigmoid(z))).astype(lhs_ref.dtype)
        return carry

    lax.fori_loop(0, tm // NORM_ROWS, chunk, 0)
    _out_proj_tail(lhs_ref, w_ref, x_ref, gn_ref, xo_ref, ho_ref)


def _nsa_out(o, z, x, w_out, g_next, final, tm=256):
    m, e = o.shape
    d = w_out.shape[1]
    out_shape, out_specs = _out_proj_outputs(m, d, tm, final)
    return pl.pallas_call(
        functools.partial(_nsa_out_body, final),
        out_shape=out_shape,
        grid=(m // tm,),
        in_specs=[pl.BlockSpec((tm, e), lambda i: (i, 0)),
                  pl.BlockSpec((tm, e), lambda i: (i, 0)),
                  pl.BlockSpec((tm, d), lambda i: (i, 0)),
                  pl.BlockSpec((e, d), lambda i: (0, 0)),
                  pl.BlockSpec((1, d), lambda i: (0, 0))],
        out_specs=out_specs,
        scratch_shapes=[pltpu.VMEM((tm, e), BF16)],
        compiler_params=_params("parallel"),
        name="nsa_out",
    )(o, z, x, w_out, g_next.reshape(1, d))


def _conformer_layer(x, h, seq, w_in, dw_w, dw_b, ln_g, ln_b, w_out, g_next, final):
    v, sz = _conformer_in(h, w_in.astype(BF16))
    return _conformer_out(v, sz, x, dw_w, dw_b, ln_g, ln_b, w_out.astype(BF16), g_next, seq, final)


def _nsa_layer(x, h, batch, seq, heads, tables, lay, w_in, cmp_pos, ck_w1, ck_w2, cv_w1, cv_w2, w_out, g_next, final):
    d = x.shape[1]
    dk = d // heads
    groups = heads // GROUP_SIZE
    q_w = heads * dk
    kv_w = groups * dk
    gate_w = 3 * heads
    assert w_in.shape[1] == 2 * q_w + 6 * kv_w + gate_w
    cuts = np.cumsum([0, q_w] + [kv_w] * 6 + [gate_w, q_w])
    col = lambda k: w_in[:, cuts[k]:cuts[k + 1]]
    w_q, w_kc, w_vc, w_ks, w_vs, w_kw, w_vw, w_g, w_z = (col(k) for k in range(9))
    w_kv = jnp.concatenate([w_kc, w_vc, w_ks, w_kw], axis=1).astype(BF16)
    w_q_t = w_q.T.astype(BF16)
    w_v_t = jnp.concatenate([w_vs, w_vw], axis=1).T.astype(BF16)
    per_group = 3 * GROUP_SIZE
    gate_rows = _gate_rows_per_group()
    w_g =jnp.pad(w_g.reshape(d, groups, per_group), ((0, 0), (0, 0), (0, gate_rows - per_group)))
    w_g_t = jnp.pad(w_g.reshape(d, groups * gate_rows).T, ((0, -groups * gate_rows % 128), (0, 0))).astype(BF16)

    kv = _kv_proj(h, w_kv, groups, dk)
    z = _matmul(h, w_z.astype(BF16))
    q_t = _proj_q_t(h, w_q_t, batch, seq, groups, dk, dk ** -0.5 * LOG2E)
    v_t = _proj_t(h, w_v_t, batch, seq, BF16, False, "nsa_in_v")
    gates_t = _proj_t(h, w_g_t, batch, seq, F32, True, "nsa_in_gates")
    kc, vct = _compress(kv, cmp_pos, ck_w1.astype(BF16), ck_w2.astype(BF16), cv_w1.astype(BF16),
                        cv_w2.T.astype(BF16), batch, seq)
    o = _attention(q_t, v_t, gates_t, kv, kc, vct, tables, batch, seq, heads, groups, dk, lay)
    return _nsa_out(o.reshape(batch * seq, q_w), z, x, w_out.astype(BF16), g_next, final)


def kernel(x, rel_bias, l0_norm, l0_w_in, l0_dw_w, l0_dw_b, l0_ln_g, l0_ln_b, l0_w_out, l1_norm, l1_w_in, l1_cmp_pos, l1_ck_w1, l1_ck_w2, l1_cv_w1, l1_cv_w2, l1_w_out, l2_norm, l2_w_in, l2_dw_w, l2_dw_b, l2_ln_g, l2_ln_b, l2_w_out, l3_norm, l3_w_in, l3_cmp_pos, l3_ck_w1, l3_ck_w2, l3_cv_w1, l3_cv_w2, l3_w_out, final_norm):
    batch, seq, d = x.shape
    heads = rel_bias.shape[1]
    lay = _table_layout(seq)
    tables = _bias_tables(rel_bias, seq, l1_cmp_pos.shape[0])
    x2 = x.reshape(batch * seq, d)
    h = _rmsnorm(x2, l0_norm)
    x2, h = _conformer_layer(x2, h, seq, l0_w_in, l0_dw_w, l0_dw_b, l0_ln_g, l0_ln_b, l0_w_out, l1_norm, False)
    x2, h = _nsa_layer(x2, h, batch, seq, heads, tables, lay, l1_w_in, l1_cmp_pos, l1_ck_w1, l1_ck_w2, l1_cv_w1,
                       l1_cv_w2, l1_w_out, l2_norm, False)
    x2, h = _conformer_layer(x2, h, seq, l2_w_in, l2_dw_w, l2_dw_b, l2_ln_g, l2_ln_b, l2_w_out, l3_norm, False)
    y = _nsa_layer(x2, h, batch, seq, heads, tables, lay, l3_w_in, l3_cmp_pos, l3_ck_w1, l3_ck_w2, l3_cv_w1,
                   l3_cv_w2, l3_w_out, final_norm, True)
    return y.reshape(batch, seq, d)
```

```python
import functools
import math

import jax
import jax.numpy as jnp
import numpy as np
from jax import lax
from jax.experimental import pallas as pl
from jax.experimental.pallas import tpu as pltpu

F32 = jnp.float32
BF16 = jnp.bfloat16

GROUP_SIZE = 4
CMP_STRIDE = 16
SEL_LEN = 64
N_SELECT = 8
WINDOW = 512
MAX_DISTANCE = 128
EPS = 1e-6
LOG2E = math.log2(math.e)
NEG = -1e30
FORCE = 1e6

TQ = 256
PROJ_ROWS = 1024
KEY_CHUNK = 64
V7X_VMEM_LIMIT_BYTES = 56 * 1024 * 1024
SUBLANES = 8
NORM_ROWS = 32


def _params(*semantics):
    return pltpu.CompilerParams(dimension_semantics=semantics, vmem_limit_bytes=V7X_VMEM_LIMIT_BYTES)


def _sigmoid(x):
    return 1.0 / (1.0 + jnp.exp(-x))


def _dot(a, b):
    return jnp.dot(a, b, preferred_element_type=F32)


def _dot_nt(a, b):
    return lax.dot_general(a, b, (((1,), (1,)), ((), ())), preferred_element_type=F32)


def _rmsnorm_body(x_ref, g_ref, o_ref):
    x = x_ref[...]
    ms = jnp.mean(x * x, axis=-1, keepdims=True)
    o_ref[...] = (x * lax.rsqrt(ms + EPS) * g_ref[...]).astype(o_ref.dtype)


def _rmsnorm(x2d, gain, tm=512):
    m, d = x2d.shape
    return pl.pallas_call(
        _rmsnorm_body,
        out_shape=jax.ShapeDtypeStruct((m, d), BF16),
        grid=(m // tm,),
        in_specs=[pl.BlockSpec((tm, d), lambda i: (i, 0)), pl.BlockSpec((1, d), lambda i: (0, 0))],
        out_specs=pl.BlockSpec((tm, d), lambda i: (i, 0)),
        compiler_params=_params("parallel"),
        name="rmsnorm",
    )(x2d, gain.reshape(1, d))


def _out_proj_tail(lhs_ref, w_ref, x_ref, gn_ref, xo_ref, ho_ref):
    tm = xo_ref.shape[0]
    xo_ref[...] = x_ref[...] + _dot(lhs_ref[...], w_ref[...])

    def chunk(c, carry):
        r0 = pl.multiple_of(c * NORM_ROWS, NORM_ROWS)
        xn = xo_ref[pl.ds(r0, NORM_ROWS), :]
        ms = jnp.mean(xn * xn, axis=-1, keepdims=True)
        hn = xn * lax.rsqrt(ms + EPS) * gn_ref[...]
        if ho_ref is None:
            xo_ref[pl.ds(r0, NORM_ROWS), :] = hn
        else:
            ho_ref[pl.ds(r0, NORM_ROWS), :] = hn.astype(ho_ref.dtype)
        return carry

    lax.fori_loop(0, tm // NORM_ROWS, chunk, 0)


def _out_proj_outputs(m, d, tm, final):
    x_spec = pl.BlockSpec((tm, d), lambda i: (i, 0))
    if final:
        return jax.ShapeDtypeStruct((m, d), F32), x_spec
    return ((jax.ShapeDtypeStruct((m, d), F32), jax.ShapeDtypeStruct((m, d), BF16)), (x_spec, x_spec))


def _conformer_in_body(h_ref, wa_ref, wb_ref, wz_ref, v_ref, sz_ref):
    h = h_ref[...]
    a = _dot(h, wa_ref[...])
    b = _dot(h, wb_ref[...])
    z = _dot(h, wz_ref[...])
    v_ref[...] = (a * _sigmoid(b)).astype(v_ref.dtype)
    sz_ref[...] = (z * _sigmoid(z)).astype(sz_ref.dtype)


def _conformer_in(h, w_in, tm=PROJ_ROWS, tn=512):
    m, d = h.shape
    e = w_in.shape[1] // 3
    nj = e // tn
    h_spec = pl.BlockSpec((tm, d), lambda i, j: (i, 0))
    o_spec = pl.BlockSpec((tm, tn), lambda i, j: (i, j))
    return pl.pallas_call(
        _conformer_in_body,
        out_shape=(jax.ShapeDtypeStruct((m, e), F32), jax.ShapeDtypeStruct((m, e), BF16)),
        grid=(m // tm, nj),
        in_specs=[h_spec,
                  pl.BlockSpec((d, tn), lambda i, j: (0, j)),
                  pl.BlockSpec((d, tn), lambda i, j: (0, j + nj)),
                  pl.BlockSpec((d, tn), lambda i, j: (0, j + 2 * nj))],
        out_specs=(o_spec, o_spec),
        compiler_params=_params("parallel", "parallel"),
        name="conformer_in",
    )(h, w_in, w_in, w_in)


CONV_HALO = 32
CONV_ROWS = 16
CONV_COLS = 512


def _conformer_out_body(tiles_per_seq, final, v_ref, halo_ref, sz_ref, x_ref, dww_ref, dwb_ref, lng_ref, lnb_ref,
                        w_ref, gn_ref, *rest):
    if final:
        xo_ref, vb_ref, sh_ref, y_ref, lhs_ref = rest
        ho_ref = None
    else:
        xo_ref, ho_ref, vb_ref, sh_ref, y_ref, lhs_ref = rest
    tm, e = v_ref.shape
    width = dww_ref.shape[0]
    first = (pl.program_id(0) % tiles_per_seq) == 0
    vb_ref[0:CONV_HALO, :] = jnp.where(first, 0.0, halo_ref[...])
    vb_ref[CONV_HALO:CONV_HALO + tm, :] = v_ref[...]
    vb_ref[CONV_HALO + tm:CONV_HALO + tm + SUBLANES, :] = jnp.zeros((SUBLANES, e), F32)

    for cb in range(e // CONV_COLS):
        cs = slice(cb * CONV_COLS, (cb + 1) * CONV_COLS)

        def shift_chunk(c, carry, cs=cs):
            r0 = pl.multiple_of(c * CONV_ROWS, CONV_ROWS)
            window = vb_ref[pl.ds(r0, CONV_ROWS + SUBLANES), cs]
            for r in range(1, SUBLANES):
                sh_ref[r - 1, pl.ds(r0, CONV_ROWS), :] = window[r:r + CONV_ROWS, :]
            return carry

        lax.fori_loop(0, (CONV_HALO + tm) // CONV_ROWS, shift_chunk, 0)

        def conv_chunk(c, carry, cs=cs):
            r0 = pl.multiple_of(c * CONV_ROWS, CONV_ROWS)
            acc = jnp.broadcast_to(dwb_ref[:, cs], (CONV_ROWS, CONV_COLS))
            for k in range(width):
                groups, r = divmod(CONV_HALO - (width - 1) + k, SUBLANES)
                rows = pl.ds(pl.multiple_of(r0 + groups * SUBLANES, SUBLANES), CONV_ROWS)
                src = vb_ref[rows, cs] if r == 0 else sh_ref[r - 1, rows, :]
                acc = acc + src * jnp.tile(dww_ref[k, :, cs], (CONV_ROWS // SUBLANES, 1))
            y_ref[pl.ds(r0, CONV_ROWS), cs] = acc
            return carry

        lax.fori_loop(0, tm // CONV_ROWS, conv_chunk, 0)

    def ln_chunk(c, carry):
        r0 = pl.multiple_of(c * NORM_ROWS, NORM_ROWS)
        y = y_ref[pl.ds(r0, NORM_ROWS), :]
        mu = jnp.mean(y, axis=-1, keepdims=True)
        dlt = y - mu
        var = jnp.mean(dlt * dlt, axis=-1, keepdims=True)
        yn = dlt * lax.rsqrt(var + EPS) * lng_ref[...] + lnb_ref[...]
        act = yn * _sigmoid(yn) * sz_ref[pl.ds(r0, NORM_ROWS), :].astype(F32)
        lhs_ref[pl.ds(r0, NORM_ROWS), :] = act.astype(lhs_ref.dtype)
        return carry

    lax.fori_loop(0, tm // NORM_ROWS, ln_chunk, 0)
    _out_proj_tail(lhs_ref, w_ref, x_ref, gn_ref, xo_ref, ho_ref)


def _conformer_out(v, sz, x, dw_w, dw_b, ln_g, ln_b, w_out, g_next, seq, final, tm=256):
    m, e = v.shape
    d = w_out.shape[1]
    width = dw_w.shape[0]
    assert width - 1 <= CONV_HALO and seq % tm == 0 and tm % CONV_HALO == 0
    halo_blocks = tm // CONV_HALO
    row = lambda a: a.reshape(1, -1)
    out_shape, out_specs = _out_proj_outputs(m, d, tm, final)
    return pl.pallas_call(
        functools.partial(_conformer_out_body, seq // tm, final),
        out_shape=out_shape,
        grid=(m // tm,),
        in_specs=[pl.BlockSpec((tm, e), lambda i: (i, 0)),
                  pl.BlockSpec((CONV_HALO, e), lambda i: (jnp.maximum(i * halo_blocks - 1, 0), 0)),
                  pl.BlockSpec((tm, e), lambda i: (i, 0)),
                  pl.BlockSpec((tm, d), lambda i: (i, 0)),
                  pl.BlockSpec((width, SUBLANES, e), lambda i: (0, 0, 0)),
                  pl.BlockSpec((1, e), lambda i: (0, 0)),
                  pl.BlockSpec((1, e), lambda i: (0, 0)),
                  pl.BlockSpec((1, e), lambda i: (0, 0)),
                  pl.BlockSpec((e, d), lambda i: (0, 0)),
                  pl.BlockSpec((1, d), lambda i: (0, 0))],
        out_specs=out_specs,
        scratch_shapes=[pltpu.VMEM((CONV_HALO + tm + SUBLANES, e), F32),
                        pltpu.VMEM((SUBLANES - 1, CONV_HALO + tm, CONV_COLS), F32),
                        pltpu.VMEM((tm, e), F32), pltpu.VMEM((tm, e), BF16)],
        compiler_params=_params("parallel"),
        name="conformer_out",
    )(v, v, sz, x, jnp.broadcast_to(dw_w[:, None, :], (width, SUBLANES, e)), row(dw_b), row(ln_g), row(ln_b), w_out, row(g_next))


def _matmul_body(h_ref, w_ref, o_ref):
    o_ref[...] = _dot(h_ref[...], w_ref[...]).astype(o_ref.dtype)


def _matmul(h, w, tm=PROJ_ROWS, tn=512):
    m, d = h.shape
    n = w.shape[1]
    return pl.pallas_call(
        _matmul_body,
        out_shape=jax.ShapeDtypeStruct((m, n), BF16),
        grid=(m // tm, n // tn),
        in_specs=[pl.BlockSpec((tm, d), lambda i, j: (i, 0)), pl.BlockSpec((d, tn), lambda i, j: (0, j))],
        out_specs=pl.BlockSpec((tm, tn), lambda i, j: (i, j)),
        compiler_params=_params("parallel", "parallel"),
        name="nsa_in_z",
    )(h, w)


def _kv_proj_body(h_ref, w_ref, o_ref):
    groups, _, dk = o_ref.shape[1:]
    res = _dot(h_ref[...], w_ref[...]).astype(o_ref.dtype)
    for g in range(groups):
        o_ref[0, g] = res[:, g * dk:(g + 1) * dk]


def _kv_proj(h, w, groups, dk, tm=PROJ_ROWS):
    m, d = h.shape
    kinds = w.shape[1] // (groups * dk)
    return pl.pallas_call(
        _kv_proj_body,
        out_shape=jax.ShapeDtypeStruct((kinds, groups, m, dk), BF16),
        grid=(m // tm, kinds),
        in_specs=[pl.BlockSpec((tm, d), lambda i, j: (i, 0)), pl.BlockSpec((d, groups * dk), lambda i, j: (0, j))],
        out_specs=pl.BlockSpec((1, groups, tm, dk), lambda i, j: (j, 0, i, 0)),
        compiler_params=_params("parallel", "parallel"),
        name="nsa_in_kv",
    )(h, w)


def _proj_q_t_body(scale, w_ref, h_ref, o_ref):
    dk = o_ref.shape[3]
    res = (_dot_nt(w_ref[...], h_ref[...]) * scale).astype(o_ref.dtype)
    for c in range(o_ref.shape[1]):
        for r in range(w_ref.shape[0] // dk):
            o_ref[0, c, 0, :, r * TQ:(r + 1) * TQ] = res[r * dk:(r + 1) * dk, c * TQ:(c + 1) * TQ]


def _proj_q_t(h, w_t, batch, seq, groups, dk, scale, tm=PROJ_ROWS):
    m, d = h.shape
    rows = w_t.shape[0] // groups
    tiles = seq // tm
    return pl.pallas_call(
        functools.partial(_proj_q_t_body, scale),
        out_shape=jax.ShapeDtypeStruct((batch, seq // TQ, groups, dk, rows // dk * TQ), BF16),
        grid=(m // tm, groups),
        in_specs=[pl.BlockSpec((rows, d), lambda i, g: (g, 0)), pl.BlockSpec((tm, d), lambda i, g: (i, 0))],
        out_specs=pl.BlockSpec((1, tm // TQ, 1, dk, rows // dk * TQ), lambda i, g: (i // tiles, i % tiles, g, 0, 0)),
        compiler_params=_params("parallel", "parallel"),
        name="nsa_in_q",
    )(w_t, h)


def _proj_t_body(sigmoid, w_ref, h_ref, o_ref):
    res = _dot_nt(w_ref[...], h_ref[...])
    if sigmoid:
        res = _sigmoid(res)
    res = res.astype(o_ref.dtype)
    for c in range(o_ref.shape[1]):
        o_ref[0, c] = res[:, c * TQ:(c + 1) * TQ]


def _proj_t(h, w_t, batch, seq, out_dtype, sigmoid, name, tm=PROJ_ROWS):
    m, d = h.shape
    rows = w_t.shape[0]
    tn = next(t for t in (512, 256, 128) if rows % t == 0)
    tiles = seq // tm
    return pl.pallas_call(
        functools.partial(_proj_t_body, sigmoid),
        out_shape=jax.ShapeDtypeStruct((batch, seq // TQ, rows, TQ), out_dtype),
        grid=(m // tm, rows // tn),
        in_specs=[pl.BlockSpec((tn, d), lambda i, j: (j, 0)), pl.BlockSpec((tm, d), lambda i, j: (i, 0))],
        out_specs=pl.BlockSpec((1, tm // TQ, tn, TQ), lambda i, j: (i // tiles, i % tiles, j, 0)),
        compiler_params=_params("parallel", "parallel"),
        name=name,
    )(w_t, h)


def _compress_body(xk_ref, xv_ref, pos_ref, w1k_ref, w2k_ref, w1v_ref, w2v_ref, kc_ref, vct_ref):
    slots = xk_ref.shape[2]

    def hidden(x_ref, w1_ref):
        x = x_ref[0, 0].astype(F32)
        first = _dot((x + pos_ref[0]).astype(BF16), w1_ref[0])
        second = _dot((x + pos_ref[1]).astype(BF16), w1_ref[1])
        pre = first + pltpu.roll(second, slots - 1, 0)
        return (pre * _sigmoid(pre)).astype(BF16)

    kc_ref[0, 0] = _dot(hidden(xk_ref, w1k_ref), w2k_ref[...]).astype(kc_ref.dtype)
    vct_ref[0, 0] = _dot_nt(w2v_ref[...], hidden(xv_ref, w1v_ref)).astype(vct_ref.dtype)


def _compress(kv, pos, w1k, w2k, w1v, w2v, batch, seq):
    kinds, groups, m, dk = kv.shape
    slots = seq // CMP_STRIDE
    assert pos.shape[0] == 2 * CMP_STRIDE
    x = kv.reshape(kinds, groups, m // CMP_STRIDE, CMP_STRIDE * dk)
    pos8 = pos.reshape(2, 1, CMP_STRIDE * dk)
    w1k = w1k.reshape(2, CMP_STRIDE * dk, dk)
    w1v = w1v.reshape(2, CMP_STRIDE * dk, dk)
    full = lambda a: pl.BlockSpec(a.shape, lambda b, g: (0,) * a.ndim)
    return pl.pallas_call(
        _compress_body,
        out_shape=(jax.ShapeDtypeStruct((batch, groups, slots, dk), BF16),
                   jax.ShapeDtypeStruct((batch, groups, dk, slots), BF16)),
        grid=(batch, groups),
        in_specs=[pl.BlockSpec((1, 1, slots, CMP_STRIDE * dk), lambda b, g: (0, g, b, 0)),
                  pl.BlockSpec((1, 1, slots, CMP_STRIDE * dk), lambda b, g: (1, g, b, 0)),
                  full(pos8), full(w1k), full(w2k), full(w1v), full(w2v)],
        out_specs=(pl.BlockSpec((1, 1, slots, dk), lambda b, g: (b, g, 0, 0)),
                   pl.BlockSpec((1, 1, dk, slots), lambda b, g: (b, g, 0, 0))),
        compiler_params=_params("parallel", "parallel"),
        name="nsa_compress",
    )(x, x, pos8, w1k, w2k, w1v, w2v)


def _t5_bucket_np(dist, num_buckets):
    dist = np.maximum(dist, 0)
    max_exact = num_buckets // 2
    d = np.maximum(dist, max_exact).astype(np.float32)
    ratio = np.log(d / np.float32(max_exact)) / np.float32(math.log(MAX_DISTANCE / max_exact))
    large = max_exact + (ratio * np.float32(num_buckets - max_exact)).astype(np.int32)
    large = np.minimum(large, num_buckets - 1)
    return np.where(dist < max_exact, dist, large).astype(np.int32)


def _table_layout(seq):
    slots = seq // CMP_STRIDE
    cmp_rows = slots + (seq // TQ - 1) * (TQ // CMP_STRIDE)
    cmp_rows = -(-cmp_rows // 64) * 64
    return dict(win_far=0, far=TQ, prev=2 * TQ, diag=3 * TQ, cmp=4 * TQ, rows=4 * TQ + cmp_rows, cmp_rows=cmp_rows)


def _bucket_index_tables(seq, num_buckets, cmp_len):
    lay = _table_layout(seq)
    j = np.arange(TQ)[:, None]
    i = np.arange(TQ)[None, :]
    far_bucket = _t5_bucket_np(np.full((TQ, TQ), 2 * TQ), num_buckets)
    assert TQ >= MAX_DISTANCE
    win_far = np.where(i < j, far_bucket, -1)
    far = far_bucket
    prev = _t5_bucket_np(i - j + TQ, num_buckets)
    diag = np.where(i >= j, _t5_bucket_np(i - j, num_buckets), -1)
    rho = np.arange(lay["cmp_rows"])[:, None]
    rel_slot = rho - (seq // TQ - 1) * (TQ // CMP_STRIDE)
    dist_c = i - CMP_STRIDE * rel_slot - (cmp_len - 1)
    cmp = np.where(dist_c >= 0, _t5_bucket_np(dist_c, num_buckets), -1)
    return np.concatenate([win_far, far, prev, diag, cmp], axis=0).astype(np.int32)


TABLE_CHUNK = 64


def _bias_table_body(rb_ref, idx_ref, o_ref):
    head = pl.program_id(0)
    num_buckets = rb_ref.shape[0]

    def chunk(c, carry):
        r0 = pl.multiple_of(c * TABLE_CHUNK, TABLE_CHUNK)
        idx = idx_ref[pl.ds(r0, TABLE_CHUNK), :]
        acc = jnp.full(idx.shape, NEG, F32)
        for k in range(num_buckets):
            acc = jnp.where(idx == k, rb_ref[k, head] * LOG2E, acc)
        o_ref[0, pl.ds(r0, TABLE_CHUNK), :] = acc
        return carry

    lax.fori_loop(0, idx_ref.shape[0] // TABLE_CHUNK, chunk, 0)


def _bias_tables(rel_bias, seq, cmp_len):
    num_buckets, heads = rel_bias.shape
    idx = jnp.asarray(_bucket_index_tables(seq, num_buckets, cmp_len))
    rows = idx.shape[0]
    return pl.pallas_call(
        _bias_table_body,
        out_shape=jax.ShapeDtypeStruct((heads, rows, TQ), F32),
        grid=(heads,),
        in_specs=[pl.BlockSpec(memory_space=pltpu.SMEM), pl.BlockSpec((rows, TQ), lambda h: (0, 0))],
        out_specs=pl.BlockSpec((1, rows, TQ), lambda h: (h, 0, 0)),
        compiler_params=_params("parallel"),
        name="t5_bias_tables",
    )(rel_bias, idx)


def _attention_body(seq, lay, qt_ref, kc_ref, vct_ref, ks_ref, kw_ref, vst_ref, vwt_ref, gt_ref, tab_ref, ovl_ref,
                    o_ref, oc_ref, accs_ref, accw_ref, selb_ref, *buffers):
    qb = pl.program_id(2)
    heads_per_group = tab_ref.shape[0]
    dk = kc_ref.shape[3]
    slots = kc_ref.shape[2]
    n_blocks = ovl_ref.shape[0]
    n_tiles = seq // TQ
    slots_per_tile = TQ // CMP_STRIDE
    blocks_per_tile = TQ // SEL_LEN
    sel_shift = SEL_LEN.bit_length() - 1

    heads = range(heads_per_group)
    lanes = [slice(r * TQ, (r + 1) * TQ) for r in heads]
    q_all = qt_ref[0, 0, 0]

    cmp_row0 = lay["cmp"] + pl.multiple_of((n_tiles - 1 - qb) * slots_per_tile, slots_per_tile)
    s_cmp = _dot(kc_ref[0, 0], q_all)
    p_sum = jnp.zeros((slots, TQ), F32)
    p_cmp = []
    for r in heads:
        s = s_cmp[:, lanes[r]] + tab_ref[r, pl.ds(cmp_row0, slots), :]
        valid = s > 0.5 * NEG
        m = jnp.max(s, axis=0, keepdims=True)
        p = jnp.where(valid, jnp.exp2(s - m), 0.0)
        l = jnp.sum(p, axis=0, keepdims=True)
        p = p * jnp.where(l > 0.0, 1.0 / l, 0.0)
        p_sum = p_sum + p
        p_cmp.append(p.astype(BF16))
    oc_ref[...] = _dot(vct_ref[0, 0], jnp.concatenate(p_cmp, axis=1))

    p_hi = p_sum.astype(BF16)
    p_lo = (p_sum - p_hi.astype(F32)).astype(BF16)
    ovl = ovl_ref[...]
    imp = _dot(ovl, p_hi) + _dot(ovl, p_lo)
    blk = lax.broadcasted_iota(jnp.int32, (n_blocks, TQ), 0)
    tq = qb * TQ + lax.broadcasted_iota(jnp.int32, (n_blocks, TQ), 1)
    cur = tq >> sel_shift
    forced = (blk == 0) | (blk == cur) | (blk == cur - 1)
    imp = jnp.where(forced, FORCE, imp)
    imp = jnp.where((blk << sel_shift) <= tq, imp, NEG)
    selb = jnp.full((n_blocks, TQ), NEG, F32)
    for _ in range(min(N_SELECT, n_blocks)):
        top = jnp.max(imp, axis=0, keepdims=True)
        first = jnp.min(jnp.where(imp == top, blk, n_blocks), axis=0, keepdims=True)
        pick = blk == first
        selb = jnp.where(pick, 0.0, selb)
        imp = jnp.where(pick, -jnp.inf, imp)
    for b in range(n_blocks):
        selb_ref[b] = jnp.broadcast_to(selb[b:b + 1, :], (SUBLANES, TQ))

    chunks = [slice(c, c + KEY_CHUNK) for c in range(0, TQ, KEY_CHUNK)]

    def key_rows(ref, kt):
        return ref[0, 0, pl.ds(pl.multiple_of(kt * TQ, TQ), TQ), :]

    def sel_bias(kt):
        tile = jnp.clip(kt - qb + 2, 0, 2)
        row0 = pl.multiple_of(lay["far"] + tile * TQ, TQ)
        masks = [jnp.tile(selb_ref[kt * blocks_per_tile + c.start // SEL_LEN], (KEY_CHUNK // SUBLANES, 1))
                 for c in chunks]
        return lambda r, n: tab_ref[r, pl.ds(row0 + chunks[n].start, KEY_CHUNK), :] + masks[n]

    def win_bias(it):
        row0 = pl.multiple_of(jnp.where(it == 0, lay["diag"], jnp.where(it == 1, lay["prev"], lay["win_far"])), TQ)
        return lambda r, n: tab_ref[r, pl.ds(row0 + chunks[n].start, KEY_CHUNK), :]

    def branch(which, n_steps, keys_of, value_of, bias_of, acc_ref):
        last = n_steps - 1
        s_refs = buffers[4 * which:4 * which + 2]
        p_refs = buffers[4 * which + 2:4 * which + 4]

        def scores(i, s_ref):
            k = keys_of(jnp.minimum(i, last))
            for r in heads:
                s_ref[r] = _dot(k, q_all[:, lanes[r]])

        def accumulate(i, alphas, p_ref):
            v_t = value_of(jnp.maximum(i, 0))
            for r in heads:
                acc_ref[r] = alphas[r] * acc_ref[r] + _dot(v_t, p_ref[r])

        acc_ref[...] = jnp.zeros(acc_ref.shape, F32)
        p_refs[1][...] = jnp.zeros(p_refs[1].shape, BF16)
        scores(0, s_refs[0])

        def half_step(parity, i, carry):
            s_ref, p_ref = s_refs[parity], p_refs[parity]
            ms, ls, alphas = carry
            accumulate(i - 1, alphas, p_refs[1 - parity])
            bias = bias_of(i)
            new_m, new_l, new_a = [], [], []
            for r in heads:
                m = ms[r]
                for n, c in enumerate(chunks):
                    s = s_ref[r, c, :] + bias(r, n)
                    s_ref[r, c, :] = s
                    m = jnp.maximum(m, jnp.max(s, axis=0, keepdims=True))
                alpha = jnp.exp2(ms[r] - m)
                l = alpha * ls[r]
                for c in chunks:
                    p = jnp.exp2(s_ref[r, c, :] - m)
                    l = l + jnp.sum(p, axis=0, keepdims=True)
                    p_ref[r, c, :] = p.astype(BF16)
                new_m.append(m)
                new_l.append(l)
                new_a.append(alpha)
            scores(i + 1, s_refs[1 - parity])
            return tuple(new_m), tuple(new_l), tuple(new_a)

        def step(i, carry):
            return lax.cond(i % 2 == 0, functools.partial(half_step, 0, i), functools.partial(half_step, 1, i), carry)

        init = (tuple(jnp.full((1, TQ), NEG, F32) for _ in heads), tuple(jnp.zeros((1, TQ), F32) for _ in heads),
                tuple(jnp.ones((1, TQ), F32) for _ in heads))
        _, ls, alphas = lax.fori_loop(0, n_steps, step, init)

        @pl.when(last % 2 == 0)
        def _():
            accumulate(last, alphas, p_refs[0])

        @pl.when(last % 2 == 1)
        def _():
            accumulate(last, alphas, p_refs[1])

        return ls

    l_s = branch(0, qb + 1, lambda kt: key_rows(ks_ref, kt), lambda kt: vst_ref[0, kt], sel_bias, accs_ref)
    l_w = branch(1, jnp.minimum(qb, WINDOW // TQ) + 1, lambda it: key_rows(kw_ref, qb - it),
                 lambda it: vwt_ref[0, qb - it], win_bias, accw_ref)

    gates = gt_ref[0, 0]
    for r in heads:
        g_c = gates[3 * r:3 * r + 1, :]
        g_s = gates[3 * r + 1:3 * r + 2, :]
        g_w = gates[3 * r + 2:3 * r + 3, :]
        out_t = g_c * oc_ref[:, lanes[r]] + (g_s / l_s[r]) * accs_ref[r] + (g_w / l_w[r]) * accw_ref[r]
        o_ref[0, :, r * dk:(r + 1) * dk] = out_t.T.astype(o_ref.dtype)


def _gate_rows_per_group():
    return -(-3 * GROUP_SIZE // SUBLANES) * SUBLANES


def _attention(q_t, v_t, gates_t, kv, kc, vct, tables, batch, seq, heads, groups, dk, lay):
    assert WINDOW == 2 * TQ and SEL_LEN % KEY_CHUNK == 0 and seq % TQ == 0
    r = heads // groups
    n_tiles = seq // TQ
    slots = seq // CMP_STRIDE
    n_blocks = seq // SEL_LEN
    gate_rows = _gate_rows_per_group()
    n0 = np.arange(slots)[None, :] * CMP_STRIDE
    s0 = np.arange(n_blocks)[:, None] * SEL_LEN
    ovl = jnp.asarray(((n0 < s0 + SEL_LEN) & (n0 + 2 * CMP_STRIDE > s0) & (n0 + 2 * CMP_STRIDE <= seq)), dtype=BF16)
    return pl.pallas_call(
        functools.partial(_attention_body, seq, lay),
        out_shape=jax.ShapeDtypeStruct((batch, seq, heads * dk), BF16),
        grid=(batch, groups, n_tiles),
        in_specs=[
            pl.BlockSpec((1, 1, 1, dk, r * TQ), lambda b, g, t: (b, t, g, 0, 0)),
            pl.BlockSpec((1, 1, slots, dk), lambda b, g, t: (b, g, 0, 0)),
            pl.BlockSpec((1, 1, dk, slots), lambda b, g, t: (b, g, 0, 0)),
            pl.BlockSpec((1, 1, seq, dk), lambda b, g, t: (2, g, b, 0)),
            pl.BlockSpec((1, 1, seq, dk), lambda b, g, t: (3, g, b, 0)),
            pl.BlockSpec((1, n_tiles, dk, TQ), lambda b, g, t: (b, 0, g, 0)),
            pl.BlockSpec((1, n_tiles, dk, TQ), lambda b, g, t: (b, 0, groups + g, 0)),
            pl.BlockSpec((1, 1, gate_rows, TQ), lambda b, g, t: (b, t, g, 0)),
            pl.BlockSpec((r, lay["rows"], TQ), lambda b, g, t: (g, 0, 0)),
            pl.BlockSpec((n_blocks, slots), lambda b, g, t: (0, 0)),
        ],
        out_specs=pl.BlockSpec((1, TQ, r * dk), lambda b, g, t: (b, t, g)),
        scratch_shapes=[pltpu.VMEM((dk, r * TQ), F32), pltpu.VMEM((r, dk, TQ), F32), pltpu.VMEM((r, dk, TQ), F32),
                        pltpu.VMEM((n_blocks, SUBLANES, TQ), F32),
                        *([pltpu.VMEM((r, TQ, TQ), F32)] * 2 + [pltpu.VMEM((r, TQ, TQ), BF16)] * 2) * 2],
        compiler_params=_params("parallel", "parallel", "arbitrary"),
        name="nsa_attention",
    )(q_t, kc, vct, kv, kv, v_t, v_t, gates_t, tables, ovl)


def _nsa_out_body(final, o_ref, z_ref, x_ref, w_ref, gn_ref, *rest):
    if final:
        xo_ref, lhs_ref = rest
        ho_ref = None
    else:
        xo_ref, ho_ref, lhs_ref = rest
    tm = o_ref.shape[0]

    def chunk(c, carry):
        r0 = pl.multiple_of(c * NORM_ROWS, NORM_ROWS)
        z = z_ref[pl.ds(r0, NORM_ROWS), :].astype(F32)
        o = o_ref[pl.ds(r0, NORM_ROWS), :].astype(F32)
        lhs_ref[pl.ds(r0, NORM_ROWS), :] = (o * (z * _sigmoid(z))).astype(lhs_ref.dtype)
        return carry

    lax.fori_loop(0, tm // NORM_ROWS, chunk, 0)
    _out_proj_tail(lhs_ref, w_ref, x_ref, gn_ref, xo_ref, ho_ref)


def _nsa_out(o, z, x, w_out, g_next, final, tm=256):
    m, e = o.shape
    d = w_out.shape[1]
    out_shape, out_specs = _out_proj_outputs(m, d, tm, final)
    return pl.pallas_call(
        functools.partial(_nsa_out_body, final),
        out_shape=out_shape,
        grid=(m // tm,),
        in_specs=[pl.BlockSpec((tm, e), lambda i: (i, 0)),
                  pl.BlockSpec((tm, e), lambda i: (i, 0)),
                  pl.BlockSpec((tm, d), lambda i: (i, 0)),
                  pl.BlockSpec((e, d), lambda i: (0, 0)),
                  pl.BlockSpec((1, d), lambda i: (0, 0))],
        out_specs=out_specs,
        scratch_shapes=[pltpu.VMEM((tm, e), BF16)],
        compiler_params=_params("parallel"),
        name="nsa_out",
    )(o, z, x, w_out, g_next.reshape(1, d))


def _conformer_layer(x, h, seq, w_in, dw_w, dw_b, ln_g, ln_b, w_out, g_next, final):
    v, sz = _conformer_in(h, w_in.astype(BF16))
    return _conformer_out(v, sz, x, dw_w, dw_b, ln_g, ln_b, w_out.astype(BF16), g_next, seq, final)


def _nsa_layer(x, h, batch, seq, heads, tables, lay, w_in, cmp_pos, ck_w1, ck_w2, cv_w1, cv_w2, w_out, g_next, final):
    d = x.shape[1]
    dk = d // heads
    groups = heads // GROUP_SIZE
    q_w = heads * dk
    kv_w = groups * dk
    gate_w = 3 * heads
    assert w_in.shape[1] == 2 * q_w + 6 * kv_w + gate_w
    cuts = np.cumsum([0, q_w] + [kv_w] * 6 + [gate_w, q_w])
    col = lambda k: w_in[:, cuts[k]:cuts[k + 1]]
    w_q, w_kc, w_vc, w_ks, w_vs, w_kw, w_vw, w_g, w_z = (col(k) for k in range(9))
    w_kv = jnp.concatenate([w_kc, w_vc, w_ks, w_kw], axis=1).astype(BF16)
    w_q_t = w_q.T.astype(BF16)
    w_v_t = jnp.concatenate([w_vs, w_vw], axis=1).T.astype(BF16)
    per_group = 3 * GROUP_SIZE
    gate_rows = _gate_rows_per_group()
    w_g =jnp.pad(w_g.reshape(d, groups, per_group), ((0, 0), (0, 0), (0, gate_rows - per_group)))
    w_g_t = jnp.pad(w_g.reshape(d, groups * gate_rows).T, ((0, -groups * gate_rows % 128), (0, 0))).astype(BF16)

    kv = _kv_proj(h, w_kv, groups, dk)
    z = _matmul(h, w_z.astype(BF16))
    q_t = _proj_q_t(h, w_q_t, batch, seq, groups, dk, dk ** -0.5 * LOG2E)
    v_t = _proj_t(h, w_v_t, batch, seq, BF16, False, "nsa_in_v")
    gates_t = _proj_t(h, w_g_t, batch, seq, F32, True, "nsa_in_gates")
    kc, vct = _compress(kv, cmp_pos, ck_w1.astype(BF16), ck_w2.astype(BF16), cv_w1.astype(BF16),
                        cv_w2.T.astype(BF16), batch, seq)
    o = _attention(q_t, v_t, gates_t, kv, kc, vct, tables, batch, seq, heads, groups, dk, lay)
    return _nsa_out(o.reshape(batch * seq, q_w), z, x, w_out.astype(BF16), g_next, final)


def kernel(x, rel_bias, l0_norm, l0_w_in, l0_dw_w, l0_dw_b, l0_ln_g, l0_ln_b, l0_w_out, l1_norm, l1_w_in, l1_cmp_pos, l1_ck_w1, l1_ck_w2, l1_cv_w1, l1_cv_w2, l1_w_out, l2_norm, l2_w_in, l2_dw_w, l2_dw_b, l2_ln_g, l2_ln_b, l2_w_out, l3_norm, l3_w_in, l3_cmp_pos, l3_ck_w1, l3_ck_w2, l3_cv_w1, l3_cv_w2, l3_w_out, final_norm):
    batch, seq, d = x.shape
    heads = rel_bias.shape[1]
    lay = _table_layout(seq)
    tables = _bias_tables(rel_bias, seq, l1_cmp_pos.shape[0])
    x2 = x.reshape(batch * seq, d)
    h = _rmsnorm(x2, l0_norm)
    x2, h = _conformer_layer(x2, h, seq, l0_w_in, l0_dw_w, l0_dw_b, l0_ln_g, l0_ln_b, l0_w_out, l1_norm, False)
    x2, h = _nsa_layer(x2, h, batch, seq, heads, tables, lay, l1_w_in, l1_cmp_pos, l1_ck_w1, l1_ck_w2, l1_cv_w1,
                       l1_cv_w2, l1_w_out, l2_norm, False)
    x2, h = _conformer_layer(x2, h, seq, l2_w_in, l2_dw_w, l2_dw_b, l2_ln_g, l2_ln_b, l2_w_out, l3_norm, False)
    y = _nsa_layer(x2, h, batch, seq, heads, tables, lay, l3_w_in, l3_cmp_pos, l3_ck_w1, l3_ck_w2, l3_cv_w1,
                   l3_cv_w2, l3_w_out, final_norm, True)
    return y.reshape(batch, seq, d)
```

```python
import functools
import math

import jax
import jax.numpy as jnp
import numpy as np
from jax import lax
from jax.experimental import pallas as pl
from jax.experimental.pallas import tpu as pltpu

F32 = jnp.float32
BF16 = jnp.bfloat16

GROUP_SIZE = 4
CMP_STRIDE = 16
SEL_LEN = 64
N_SELECT = 8
WINDOW = 512
MAX_DISTANCE = 128
EPS = 1e-6
LOG2E = math.log2(math.e)
NEG = -1e30
FORCE = 1e6

TQ = 256
PROJ_ROWS = 1024
BRANCH_BUFFERS = 5
KEY_CHUNK = 64
V7X_VMEM_LIMIT_BYTES = 56 * 1024 * 1024
SUBLANES = 8
NORM_ROWS = 32


def _params(*semantics):
    return pltpu.CompilerParams(dimension_semantics=semantics, vmem_limit_bytes=V7X_VMEM_LIMIT_BYTES)


def _sigmoid(x):
    return 1.0 / (1.0 + jnp.exp(-x))


def _dot(a, b):
    return jnp.dot(a, b, preferred_element_type=F32)


def _dot_nt(a, b):
    return lax.dot_general(a, b, (((1,), (1,)), ((), ())), preferred_element_type=F32)


def _rmsnorm_body(x_ref, g_ref, o_ref):
    x = x_ref[...]
    ms = jnp.mean(x * x, axis=-1, keepdims=True)
    o_ref[...] = (x * lax.rsqrt(ms + EPS) * g_ref[...]).astype(o_ref.dtype)


def _rmsnorm(x2d, gain, tm=512):
    m, d = x2d.shape
    return pl.pallas_call(
        _rmsnorm_body,
        out_shape=jax.ShapeDtypeStruct((m, d), BF16),
        grid=(m // tm,),
        in_specs=[pl.BlockSpec((tm, d), lambda i: (i, 0)), pl.BlockSpec((1, d), lambda i: (0, 0))],
        out_specs=pl.BlockSpec((tm, d), lambda i: (i, 0)),
        compiler_params=_params("parallel"),
        name="rmsnorm",
    )(x2d, gain.reshape(1, d))


def _out_proj_tail(lhs_ref, w_ref, x_ref, gn_ref, xo_ref, ho_ref):
    tm = xo_ref.shape[0]
    xo_ref[...] = x_ref[...] + _dot(lhs_ref[...], w_ref[...])

    def chunk(c, carry):
        r0 = pl.multiple_of(c * NORM_ROWS, NORM_ROWS)
        xn = xo_ref[pl.ds(r0, NORM_ROWS), :]
        ms = jnp.mean(xn * xn, axis=-1, keepdims=True)
        hn = xn * lax.rsqrt(ms + EPS) * gn_ref[...]
        if ho_ref is None:
            xo_ref[pl.ds(r0, NORM_ROWS), :] = hn
        else:
            ho_ref[pl.ds(r0, NORM_ROWS), :] = hn.astype(ho_ref.dtype)
        return carry

    lax.fori_loop(0, tm // NORM_ROWS, chunk, 0)


def _out_proj_outputs(m, d, tm, final):
    x_spec = pl.BlockSpec((tm, d), lambda i: (i, 0))
    if final:
        return jax.ShapeDtypeStruct((m, d), F32), x_spec
    return ((jax.ShapeDtypeStruct((m, d), F32), jax.ShapeDtypeStruct((m, d), BF16)), (x_spec, x_spec))


def _conformer_in_body(h_ref, wa_ref, wb_ref, wz_ref, v_ref, sz_ref):
    h = h_ref[...]
    a = _dot(h, wa_ref[...])
    b = _dot(h, wb_ref[...])
    z = _dot(h, wz_ref[...])
    v_ref[...] = (a * _sigmoid(b)).astype(v_ref.dtype)
    sz_ref[...] = (z * _sigmoid(z)).astype(sz_ref.dtype)


def _conformer_in(h, w_in, tm=PROJ_ROWS, tn=512):
    m, d = h.shape
    e = w_in.shape[1] // 3
    nj = e // tn
    h_spec = pl.BlockSpec((tm, d), lambda i, j: (i, 0))
    o_spec = pl.BlockSpec((tm, tn), lambda i, j: (i, j))
    return pl.pallas_call(
        _conformer_in_body,
        out_shape=(jax.ShapeDtypeStruct((m, e), F32), jax.ShapeDtypeStruct((m, e), BF16)),
        grid=(m // tm, nj),
        in_specs=[h_spec,
                  pl.BlockSpec((d, tn), lambda i, j: (0, j)),
                  pl.BlockSpec((d, tn), lambda i, j: (0, j + nj)),
                  pl.BlockSpec((d, tn), lambda i, j: (0, j + 2 * nj))],
        out_specs=(o_spec, o_spec),
        compiler_params=_params("parallel", "parallel"),
        name="conformer_in",
    )(h, w_in, w_in, w_in)


CONV_HALO = 32
CONV_ROWS = 32
CONV_COLS = 512


def _conformer_out_body(tiles_per_seq, final, v_ref, halo_ref, sz_ref, x_ref, dww_ref, dwb_ref, lng_ref, lnb_ref,
                        w_ref, gn_ref, *rest):
    if final:
        xo_ref, vb_ref, sh_ref, y_ref, lhs_ref = rest
        ho_ref = None
    else:
        xo_ref, ho_ref, vb_ref, sh_ref, y_ref, lhs_ref = rest
    tm, e = v_ref.shape
    width = dww_ref.shape[0]
    first = (pl.program_id(0) % tiles_per_seq) == 0
    vb_ref[0:CONV_HALO, :] = jnp.where(first, 0.0, halo_ref[...])
    vb_ref[CONV_HALO:CONV_HALO + tm, :] = v_ref[...]
    vb_ref[CONV_HALO + tm:CONV_HALO + tm + SUBLANES, :] = jnp.zeros((SUBLANES, e), F32)

    for cb in range(e // CONV_COLS):
        cs = slice(cb * CONV_COLS, (cb + 1) * CONV_COLS)

        def shift_chunk(c, carry, cs=cs):
            r0 = pl.multiple_of(c * CONV_ROWS, CONV_ROWS)
            window = vb_ref[pl.ds(r0, CONV_ROWS + SUBLANES), cs]
            for r in range(1, SUBLANES):
                sh_ref[r - 1, pl.ds(r0, CONV_ROWS), :] = window[r:r + CONV_ROWS, :]
            return carry

        lax.fori_loop(0, (CONV_HALO + tm) // CONV_ROWS, shift_chunk, 0)

        def conv_chunk(c, carry, cs=cs):
            r0 = pl.multiple_of(c * CONV_ROWS, CONV_ROWS)
            acc = jnp.broadcast_to(dwb_ref[:, cs], (CONV_ROWS, CONV_COLS))
            for k in range(width):
                groups, r = divmod(CONV_HALO - (width - 1) + k, SUBLANES)
                rows = pl.ds(pl.multiple_of(r0 + groups * SUBLANES, SUBLANES), CONV_ROWS)
                src = vb_ref[rows, cs] if r == 0 else sh_ref[r - 1, rows, :]
                acc = acc + src * jnp.tile(dww_ref[k, :, cs], (CONV_ROWS // SUBLANES, 1))
            y_ref[pl.ds(r0, CONV_ROWS), cs] = acc
            return carry

        lax.fori_loop(0, tm // CONV_ROWS, conv_chunk, 0)

    def ln_chunk(c, carry):
        r0 = pl.multiple_of(c * NORM_ROWS, NORM_ROWS)
        y = y_ref[pl.ds(r0, NORM_ROWS), :]
        mu = jnp.mean(y, axis=-1, keepdims=True)
        dlt = y - mu
        var = jnp.mean(dlt * dlt, axis=-1, keepdims=True)
        yn = dlt * lax.rsqrt(var + EPS) * lng_ref[...] + lnb_ref[...]
        act = yn * _sigmoid(yn) * sz_ref[pl.ds(r0, NORM_ROWS), :].astype(F32)
        lhs_ref[pl.ds(r0, NORM_ROWS), :] = act.astype(lhs_ref.dtype)
        return carry

    lax.fori_loop(0, tm // NORM_ROWS, ln_chunk, 0)
    _out_proj_tail(lhs_ref, w_ref, x_ref, gn_ref, xo_ref, ho_ref)


def _conformer_out(v, sz, x, dw_w, dw_b, ln_g, ln_b, w_out, g_next, seq, final, tm=256):
    m, e = v.shape
    d = w_out.shape[1]
    width = dw_w.shape[0]
    assert width - 1 <= CONV_HALO and seq % tm == 0 and tm % CONV_HALO == 0
    halo_blocks = tm // CONV_HALO
    row = lambda a: a.reshape(1, -1)
    out_shape, out_specs = _out_proj_outputs(m, d, tm, final)
    return pl.pallas_call(
        functools.partial(_conformer_out_body, seq // tm, final),
        out_shape=out_shape,
        grid=(m // tm,),
        in_specs=[pl.BlockSpec((tm, e), lambda i: (i, 0)),
                  pl.BlockSpec((CONV_HALO, e), lambda i: (jnp.maximum(i * halo_blocks - 1, 0), 0)),
                  pl.BlockSpec((tm, e), lambda i: (i, 0)),
                  pl.BlockSpec((tm, d), lambda i: (i, 0)),
                  pl.BlockSpec((width, SUBLANES, e), lambda i: (0, 0, 0)),
                  pl.BlockSpec((1, e), lambda i: (0, 0)),
                  pl.BlockSpec((1, e), lambda i: (0, 0)),
                  pl.BlockSpec((1, e), lambda i: (0, 0)),
                  pl.BlockSpec((e, d), lambda i: (0, 0)),
                  pl.BlockSpec((1, d), lambda i: (0, 0))],
        out_specs=out_specs,
        scratch_shapes=[pltpu.VMEM((CONV_HALO + tm + SUBLANES, e), F32),
                        pltpu.VMEM((SUBLANES - 1, CONV_HALO + tm, CONV_COLS), F32),
                        pltpu.VMEM((tm, e), F32), pltpu.VMEM((tm, e), BF16)],
        compiler_params=_params("parallel"),
        name="conformer_out",
    )(v, v, sz, x, jnp.broadcast_to(dw_w[:, None, :], (width, SUBLANES, e)), row(dw_b), row(ln_g), row(ln_b), w_out, row(g_next))


def _matmul_body(h_ref, w_ref, o_ref):
    o_ref[...] = _dot(h_ref[...], w_ref[...]).astype(o_ref.dtype)


def _matmul(h, w, tm=PROJ_ROWS, tn=512):
    m, d = h.shape
    n = w.shape[1]
    return pl.pallas_call(
        _matmul_body,
        out_shape=jax.ShapeDtypeStruct((m, n), BF16),
        grid=(m // tm, n // tn),
        in_specs=[pl.BlockSpec((tm, d), lambda i, j: (i, 0)), pl.BlockSpec((d, tn), lambda i, j: (0, j))],
        out_specs=pl.BlockSpec((tm, tn), lambda i, j: (i, j)),
        compiler_params=_params("parallel", "parallel"),
        name="nsa_in_z",
    )(h, w)


def _kv_proj_body(h_ref, w_ref, o_ref):
    groups, _, dk = o_ref.shape[1:]
    res = _dot(h_ref[...], w_ref[...]).astype(o_ref.dtype)
    for g in range(groups):
        o_ref[0, g] = res[:, g * dk:(g + 1) * dk]


def _kv_proj(h, w, groups, dk, tm=PROJ_ROWS):
    m, d = h.shape
    kinds = w.shape[1] // (groups * dk)
    return pl.pallas_call(
        _kv_proj_body,
        out_shape=jax.ShapeDtypeStruct((kinds, groups, m, dk), BF16),
        grid=(m // tm, kinds),
        in_specs=[pl.BlockSpec((tm, d), lambda i, j: (i, 0)), pl.BlockSpec((d, groups * dk), lambda i, j: (0, j))],
        out_specs=pl.BlockSpec((1, groups, tm, dk), lambda i, j: (j, 0, i, 0)),
        compiler_params=_params("parallel", "parallel"),
        name="nsa_in_kv",
    )(h, w)


def _proj_q_t_body(scale, w_ref, h_ref, o_ref):
    dk = o_ref.shape[3]
    res = (_dot_nt(w_ref[...], h_ref[...]) * scale).astype(o_ref.dtype)
    for c in range(o_ref.shape[1]):
        for r in range(w_ref.shape[0] // dk):
            o_ref[0, c, 0, :, r * TQ:(r + 1) * TQ] = res[r * dk:(r + 1) * dk, c * TQ:(c + 1) * TQ]


def _proj_q_t(h, w_t, batch, seq, groups, dk, scale, tm=PROJ_ROWS):
    m, d = h.shape
    rows = w_t.shape[0] // groups
    tiles = seq // tm
    return pl.pallas_call(
        functools.partial(_proj_q_t_body, scale),
        out_shape=jax.ShapeDtypeStruct((batch, seq // TQ, groups, dk, rows // dk * TQ), BF16),
        grid=(m // tm, groups),
        in_specs=[pl.BlockSpec((rows, d), lambda i, g: (g, 0)), pl.BlockSpec((tm, d), lambda i, g: (i, 0))],
        out_specs=pl.BlockSpec((1, tm // TQ, 1, dk, rows // dk * TQ), lambda i, g: (i // tiles, i % tiles, g, 0, 0)),
        compiler_params=_params("parallel", "parallel"),
        name="nsa_in_q",
    )(w_t, h)


def _proj_t_body(sigmoid, w_ref, h_ref, o_ref):
    res = _dot_nt(w_ref[...], h_ref[...])
    if sigmoid:
        res = _sigmoid(res)
    res = res.astype(o_ref.dtype)
    for c in range(o_ref.shape[1]):
        o_ref[0, c] = res[:, c * TQ:(c + 1) * TQ]


def _proj_t(h, w_t, batch, seq, out_dtype, sigmoid, name, tm=PROJ_ROWS):
    m, d = h.shape
    rows = w_t.shape[0]
    tn = next(t for t in (512, 256, 128) if rows % t == 0)
    tiles = seq // tm
    return pl.pallas_call(
        functools.partial(_proj_t_body, sigmoid),
        out_shape=jax.ShapeDtypeStruct((batch, seq // TQ, rows, TQ), out_dtype),
        grid=(m // tm, rows // tn),
        in_specs=[pl.BlockSpec((tn, d), lambda i, j: (j, 0)), pl.BlockSpec((tm, d), lambda i, j: (i, 0))],
        out_specs=pl.BlockSpec((1, tm // TQ, tn, TQ), lambda i, j: (i // tiles, i % tiles, j, 0)),
        compiler_params=_params("parallel", "parallel"),
        name=name,
    )(w_t, h)


def _compress_body(xk_ref, xv_ref, pos_ref, w1k_ref, w2k_ref, w1v_ref, w2v_ref, kc_ref, vct_ref):
    slots = xk_ref.shape[2]

    def hidden(x_ref, w1_ref):
        x = x_ref[0, 0].astype(F32)
        first = _dot((x + pos_ref[0]).astype(BF16), w1_ref[0])
        second = _dot((x + pos_ref[1]).astype(BF16), w1_ref[1])
        pre = first + pltpu.roll(second, slots - 1, 0)
        return (pre * _sigmoid(pre)).astype(BF16)

    kc_ref[0, 0] = _dot(hidden(xk_ref, w1k_ref), w2k_ref[...]).astype(kc_ref.dtype)
    vct_ref[0, 0] = _dot_nt(w2v_ref[...], hidden(xv_ref, w1v_ref)).astype(vct_ref.dtype)


def _compress(kv, pos, w1k, w2k, w1v, w2v, batch, seq):
    kinds, groups, m, dk = kv.shape
    slots = seq // CMP_STRIDE
    assert pos.shape[0] == 2 * CMP_STRIDE
    x = kv.reshape(kinds, groups, m // CMP_STRIDE, CMP_STRIDE * dk)
    pos8 = pos.reshape(2, 1, CMP_STRIDE * dk)
    w1k = w1k.reshape(2, CMP_STRIDE * dk, dk)
    w1v = w1v.reshape(2, CMP_STRIDE * dk, dk)
    full = lambda a: pl.BlockSpec(a.shape, lambda b, g: (0,) * a.ndim)
    return pl.pallas_call(
        _compress_body,
        out_shape=(jax.ShapeDtypeStruct((batch, groups, slots, dk), BF16),
                   jax.ShapeDtypeStruct((batch, groups, dk, slots), BF16)),
        grid=(batch, groups),
        in_specs=[pl.BlockSpec((1, 1, slots, CMP_STRIDE * dk), lambda b, g: (0, g, b, 0)),
                  pl.BlockSpec((1, 1, slots, CMP_STRIDE * dk), lambda b, g: (1, g, b, 0)),
                  full(pos8), full(w1k), full(w2k), full(w1v), full(w2v)],
        out_specs=(pl.BlockSpec((1, 1, slots, dk), lambda b, g: (b, g, 0, 0)),
                   pl.BlockSpec((1, 1, dk, slots), lambda b, g: (b, g, 0, 0))),
        compiler_params=_params("parallel", "parallel"),
        name="nsa_compress",
    )(x, x, pos8, w1k, w2k, w1v, w2v)


def _t5_bucket_np(dist, num_buckets):
    dist = np.maximum(dist, 0)
    max_exact = num_buckets // 2
    d = np.maximum(dist, max_exact).astype(np.float32)
    ratio = np.log(d / np.float32(max_exact)) / np.float32(math.log(MAX_DISTANCE / max_exact))
    large = max_exact + (ratio * np.float32(num_buckets - max_exact)).astype(np.int32)
    large = np.minimum(large, num_buckets - 1)
    return np.where(dist < max_exact, dist, large).astype(np.int32)


def _table_layout(seq):
    slots = seq // CMP_STRIDE
    cmp_rows = slots + (seq // TQ - 1) * (TQ // CMP_STRIDE)
    cmp_rows = -(-cmp_rows // 64) * 64
    return dict(win_far=0, far=TQ, prev=2 * TQ, diag=3 * TQ, cmp=4 * TQ, rows=4 * TQ + cmp_rows, cmp_rows=cmp_rows)


def _bucket_index_tables(seq, num_buckets, cmp_len):
    lay = _table_layout(seq)
    j = np.arange(TQ)[:, None]
    i = np.arange(TQ)[None, :]
    far_bucket = _t5_bucket_np(np.full((TQ, TQ), 2 * TQ), num_buckets)
    assert TQ >= MAX_DISTANCE
    win_far = np.where(i < j, far_bucket, -1)
    far = far_bucket
    prev = _t5_bucket_np(i - j + TQ, num_buckets)
    diag = np.where(i >= j, _t5_bucket_np(i - j, num_buckets), -1)
    rho = np.arange(lay["cmp_rows"])[:, None]
    rel_slot = rho - (seq // TQ - 1) * (TQ // CMP_STRIDE)
    dist_c = i - CMP_STRIDE * rel_slot - (cmp_len - 1)
    cmp = np.where(dist_c >= 0, _t5_bucket_np(dist_c, num_buckets), -1)
    return np.concatenate([win_far, far, prev, diag, cmp], axis=0).astype(np.int32)


TABLE_CHUNK = 64


def _bias_table_body(rb_ref, idx_ref, o_ref):
    head = pl.program_id(0)
    num_buckets = rb_ref.shape[0]

    def chunk(c, carry):
        r0 = pl.multiple_of(c * TABLE_CHUNK, TABLE_CHUNK)
        idx = idx_ref[pl.ds(r0, TABLE_CHUNK), :]
        acc = jnp.full(idx.shape, NEG, F32)
        for k in range(num_buckets):
            acc = jnp.where(idx == k, rb_ref[k, head] * LOG2E, acc)
        o_ref[0, pl.ds(r0, TABLE_CHUNK), :] = acc
        return carry

    lax.fori_loop(0, idx_ref.shape[0] // TABLE_CHUNK, chunk, 0)


def _bias_tables(rel_bias, seq, cmp_len):
    num_buckets, heads = rel_bias.shape
    idx = jnp.asarray(_bucket_index_tables(seq, num_buckets, cmp_len))
    rows = idx.shape[0]
    return pl.pallas_call(
        _bias_table_body,
        out_shape=jax.ShapeDtypeStruct((heads, rows, TQ), F32),
        grid=(heads,),
        in_specs=[pl.BlockSpec(memory_space=pltpu.SMEM), pl.BlockSpec((rows, TQ), lambda h: (0, 0))],
        out_specs=pl.BlockSpec((1, rows, TQ), lambda h: (h, 0, 0)),
        compiler_params=_params("parallel"),
        name="t5_bias_tables",
    )(rel_bias, idx)


def _attention_body(seq, lay, qt_ref, kc_ref, vct_ref, ks_ref, kw_ref, vst_ref, vwt_ref, gt_ref, tab_ref, ovl_ref,
                    o_ref, oc_ref, accs_ref, accw_ref, selb_ref, *buffers):
    qb = pl.program_id(2)
    heads_per_group = tab_ref.shape[0]
    dk = kc_ref.shape[3]
    slots = kc_ref.shape[2]
    n_blocks = ovl_ref.shape[0]
    n_tiles = seq // TQ
    slots_per_tile = TQ // CMP_STRIDE
    blocks_per_tile = TQ // SEL_LEN
    sel_shift = SEL_LEN.bit_length() - 1

    heads = range(heads_per_group)
    lanes = [slice(r * TQ, (r + 1) * TQ) for r in heads]
    q_all = qt_ref[0, 0, 0]

    def key_rows(ref, kt):
        return ref[0, 0, pl.ds(pl.multiple_of(kt * TQ, TQ), TQ), :]

    def branch_buffers(which):
        base = BRANCH_BUFFERS * which
        return buffers[base:base + 2], buffers[base + 2:base + 4], buffers[base + 4]

    for which, acc_ref, first_keys in ((0, accs_ref, key_rows(ks_ref, 0)), (1, accw_ref, key_rows(kw_ref, qb))):
        s_refs, p_refs, _ = branch_buffers(which)
        acc_ref[...] = jnp.zeros(acc_ref.shape, F32)
        p_refs[1][...] = jnp.zeros(p_refs[1].shape, BF16)
        for r in heads:
            s_refs[0][r] = _dot(first_keys, q_all[:, lanes[r]])

    cmp_row0 = lay["cmp"] + pl.multiple_of((n_tiles - 1 - qb) * slots_per_tile, slots_per_tile)
    s_cmp = _dot(kc_ref[0, 0], q_all)
    p_sum = jnp.zeros((slots, TQ), F32)
    p_cmp = []
    for r in heads:
        s = s_cmp[:, lanes[r]] + tab_ref[r, pl.ds(cmp_row0, slots), :]
        valid = s > 0.5 * NEG
        m = jnp.max(s, axis=0, keepdims=True)
        p = jnp.where(valid, jnp.exp2(s - m), 0.0)
        l = jnp.sum(p, axis=0, keepdims=True)
        p = p * jnp.where(l > 0.0, 1.0 / l, 0.0)
        p_sum = p_sum + p
        p_cmp.append(p.astype(BF16))
    oc_ref[...] = _dot(vct_ref[0, 0], jnp.concatenate(p_cmp, axis=1))

    p_hi = p_sum.astype(BF16)
    p_lo = (p_sum - p_hi.astype(F32)).astype(BF16)
    ovl = ovl_ref[...]
    imp = _dot(ovl, p_hi) + _dot(ovl, p_lo)
    blk = lax.broadcasted_iota(jnp.int32, (n_blocks, TQ), 0)
    tq = qb * TQ + lax.broadcasted_iota(jnp.int32, (n_blocks, TQ), 1)
    cur = tq >> sel_shift
    forced = (blk == 0) | (blk == cur) | (blk == cur - 1)
    imp = jnp.where(forced, FORCE, imp)
    imp = jnp.where((blk << sel_shift) <= tq, imp, NEG)
    selb = jnp.full((n_blocks, TQ), NEG, F32)
    for _ in range(min(N_SELECT, n_blocks)):
        top = jnp.max(imp, axis=0, keepdims=True)
        first = jnp.min(jnp.where(imp == top, blk, n_blocks), axis=0, keepdims=True)
        pick = blk == first
        selb = jnp.where(pick, 0.0, selb)
        imp = jnp.where(pick, -jnp.inf, imp)
    for b in range(n_blocks):
        selb_ref[b] = jnp.broadcast_to(selb[b:b + 1, :], (SUBLANES, TQ))

    chunks = [slice(c, c + KEY_CHUNK) for c in range(0, TQ, KEY_CHUNK)]

    def sel_bias(kt):
        tile = jnp.clip(kt - qb + 2, 0, 2)
        row0 = pl.multiple_of(lay["far"] + tile * TQ, TQ)
        masks = [jnp.tile(selb_ref[kt * blocks_per_tile + c.start // SEL_LEN], (KEY_CHUNK // SUBLANES, 1))
                 for c in chunks]
        return lambda r, n: tab_ref[r, pl.ds(row0 + chunks[n].start, KEY_CHUNK), :] + masks[n]

    def win_bias(it):
        row0 = pl.multiple_of(jnp.where(it == 0, lay["diag"], jnp.where(it == 1, lay["prev"], lay["win_far"])), TQ)
        return lambda r, n: tab_ref[r, pl.ds(row0 + chunks[n].start, KEY_CHUNK), :]

    def branch(which, n_steps, keys_of, value_of, bias_of, acc_ref):
        last = n_steps - 1
        s_refs, p_refs, p_last_ref = branch_buffers(which)

        def scores(i, s_ref):
            k = keys_of(jnp.minimum(i, last))
            for r in heads:
                s_ref[r] = _dot(k, q_all[:, lanes[r]])

        def accumulate(i, alphas, p_ref):
            v_t = value_of(jnp.maximum(i, 0))
            for r in heads:
                acc_ref[r] = alphas[r] * acc_ref[r] + _dot(v_t, p_ref[r])

        def half_step(parity, i, carry):
            s_ref, p_ref = s_refs[parity], p_refs[parity]
            ms, ls, alphas = carry
            accumulate(i - 1, alphas, p_refs[1 - parity])
            bias = bias_of(i)
            new_m, new_l, new_a = [], [], []
            for r in heads:
                m = ms[r]
                for n, c in enumerate(chunks):
                    s = s_ref[r, c, :] + bias(r, n)
                    s_ref[r, c, :] = s
                    m = jnp.maximum(m, jnp.max(s, axis=0, keepdims=True))
                alpha = jnp.exp2(ms[r] - m)
                l = alpha * ls[r]
                for c in chunks:
                    p = jnp.exp2(s_ref[r, c, :] - m)
                    l = l + jnp.sum(p, axis=0, keepdims=True)
                    p_ref[r, c, :] = p.astype(BF16)
                    p_last_ref[r, c, :] = p.astype(BF16)
                new_m.append(m)
                new_l.append(l)
                new_a.append(alpha)
            scores(i + 1, s_refs[1 - parity])
            return tuple(new_m), tuple(new_l), tuple(new_a)

        def step(i, carry):
            return lax.cond(i % 2 == 0, functools.partial(half_step, 0, i), functools.partial(half_step, 1, i), carry)

        init = (tuple(jnp.full((1, TQ), NEG, F32) for _ in heads), tuple(jnp.zeros((1, TQ), F32) for _ in heads),
                tuple(jnp.ones((1, TQ), F32) for _ in heads))
        _, ls, alphas = lax.fori_loop(0, n_steps, step, init)
        return ls, lambda: accumulate(last, alphas, p_last_ref)

    l_s, finish_s = branch(0, qb + 1, lambda kt: key_rows(ks_ref, kt), lambda kt: vst_ref[0, kt], sel_bias, accs_ref)
    l_w, finish_w = branch(1, jnp.minimum(qb, WINDOW // TQ) + 1, lambda it: key_rows(kw_ref, qb - it),
                           lambda it: vwt_ref[0, qb - it], win_bias, accw_ref)
    finish_s()
    finish_w()

    gates = gt_ref[0, 0]
    for r in heads:
        g_c = gates[3 * r:3 * r + 1, :]
        g_s = gates[3 * r + 1:3 * r + 2, :]
        g_w = gates[3 * r + 2:3 * r + 3, :]
        out_t = g_c * oc_ref[:, lanes[r]] + (g_s / l_s[r]) * accs_ref[r] + (g_w / l_w[r]) * accw_ref[r]
        o_ref[0, :, r * dk:(r + 1) * dk] = out_t.T.astype(o_ref.dtype)


def _gate_rows_per_group():
    return -(-3 * GROUP_SIZE // SUBLANES) * SUBLANES


def _attention(q_t, v_t, gates_t, kv, kc, vct, tables, batch, seq, heads, groups, dk, lay):
    assert WINDOW == 2 * TQ and SEL_LEN % KEY_CHUNK == 0 and seq % TQ == 0
    r = heads // groups
    n_tiles = seq // TQ
    slots = seq // CMP_STRIDE
    n_blocks = seq // SEL_LEN
    gate_rows = _gate_rows_per_group()
    n0 = np.arange(slots)[None, :] * CMP_STRIDE
    s0 = np.arange(n_blocks)[:, None] * SEL_LEN
    ovl = jnp.asarray(((n0 < s0 + SEL_LEN) & (n0 + 2 * CMP_STRIDE > s0) & (n0 + 2 * CMP_STRIDE <= seq)), dtype=BF16)
    return pl.pallas_call(
        functools.partial(_attention_body, seq, lay),
        out_shape=jax.ShapeDtypeStruct((batch, seq, heads * dk), BF16),
        grid=(batch, groups, n_tiles),
        in_specs=[
            pl.BlockSpec((1, 1, 1, dk, r * TQ), lambda b, g, t: (b, t, g, 0, 0)),
            pl.BlockSpec((1, 1, slots, dk), lambda b, g, t: (b, g, 0, 0)),
            pl.BlockSpec((1, 1, dk, slots), lambda b, g, t: (b, g, 0, 0)),
            pl.BlockSpec((1, 1, seq, dk), lambda b, g, t: (2, g, b, 0)),
            pl.BlockSpec((1, 1, seq, dk), lambda b, g, t: (3, g, b, 0)),
            pl.BlockSpec((1, n_tiles, dk, TQ), lambda b, g, t: (b, 0, g, 0)),
            pl.BlockSpec((1, n_tiles, dk, TQ), lambda b, g, t: (b, 0, groups + g, 0)),
            pl.BlockSpec((1, 1, gate_rows, TQ), lambda b, g, t: (b, t, g, 0)),
            pl.BlockSpec((r, lay["rows"], TQ), lambda b, g, t: (g, 0, 0)),
            pl.BlockSpec((n_blocks, slots), lambda b, g, t: (0, 0)),
        ],
        out_specs=pl.BlockSpec((1, TQ, r * dk), lambda b, g, t: (b, t, g)),
        scratch_shapes=[pltpu.VMEM((dk, r * TQ), F32), pltpu.VMEM((r, dk, TQ), F32), pltpu.VMEM((r, dk, TQ), F32),
                        pltpu.VMEM((n_blocks, SUBLANES, TQ), F32),
                        *([pltpu.VMEM((r, TQ, TQ), F32)] * 2 + [pltpu.VMEM((r, TQ, TQ), BF16)] * 3) * 2],
        compiler_params=_params("parallel", "parallel", "arbitrary"),
        name="nsa_attention",
    )(q_t, kc, vct, kv, kv, v_t, v_t, gates_t, tables, ovl)


def _nsa_out_body(final, o_ref, z_ref, x_ref, w_ref, gn_ref, *rest):
    if final:
        xo_ref, lhs_ref = rest
        ho_ref = None
    else:
        xo_ref, ho_ref, lhs_ref = rest
    tm = o_ref.shape[0]

    def chunk(c, carry):
        r0 = pl.multiple_of(c * NORM_ROWS, NORM_ROWS)
        z = z_ref[pl.ds(r0, NORM_ROWS), :].astype(F32)
        o = o_ref[pl.ds(r0, NORM_ROWS), :].astype(F32)
        lhs_ref[pl.ds(r0, NORM_ROWS), :] = (o * (z * _sigmoid(z))).astype(lhs_ref.dtype)
        return carry

    lax.fori_loop(0, tm // NORM_ROWS, chunk, 0)
    _out_proj_tail(lhs_ref, w_ref, x_ref, gn_ref, xo_ref, ho_ref)


def _nsa_out(o, z, x, w_out, g_next, final, tm=256):
    m, e = o.shape
    d = w_out.shape[1]
    out_shape, out_specs = _out_proj_outputs(m, d, tm, final)
    return pl.pallas_call(
        functools.partial(_nsa_out_body, final),
        out_shape=out_shape,
        grid=(m // tm,),
        in_specs=[pl.BlockSpec((tm, e), lambda i: (i, 0)),
                  pl.BlockSpec((tm, e), lambda i: (i, 0)),
                  pl.BlockSpec((tm, d), lambda i: (i, 0)),
                  pl.BlockSpec((e, d), lambda i: (0, 0)),
                  pl.BlockSpec((1, d), lambda i: (0, 0))],
        out_specs=out_specs,
        scratch_shapes=[pltpu.VMEM((tm, e), BF16)],
        compiler_params=_params("parallel"),
        name="nsa_out",
    )(o, z, x, w_out, g_next.reshape(1, d))


def _conformer_layer(x, h, seq, w_in, dw_w, dw_b, ln_g, ln_b, w_out, g_next, final):
    v, sz = _conformer_in(h, w_in.astype(BF16))
    return _conformer_out(v, sz, x, dw_w, dw_b, ln_g, ln_b, w_out.astype(BF16), g_next, seq, final)


def _nsa_layer(x, h, batch, seq, heads, tables, lay, w_in, cmp_pos, ck_w1, ck_w2, cv_w1, cv_w2, w_out, g_next, final):
    d = x.shape[1]
    dk = d // heads
    groups = heads // GROUP_SIZE
    q_w = heads * dk
    kv_w = groups * dk
    gate_w = 3 * heads
    assert w_in.shape[1] == 2 * q_w + 6 * kv_w + gate_w
    cuts = np.cumsum([0, q_w] + [kv_w] * 6 + [gate_w, q_w])
    col = lambda k: w_in[:, cuts[k]:cuts[k + 1]]
    w_q, w_kc, w_vc, w_ks, w_vs, w_kw, w_vw, w_g, w_z = (col(k) for k in range(9))
    w_kv = jnp.concatenate([w_kc, w_vc, w_ks, w_kw], axis=1).astype(BF16)
    w_q_t = w_q.T.astype(BF16)
    w_v_t = jnp.concatenate([w_vs, w_vw], axis=1).T.astype(BF16)
    per_group = 3 * GROUP_SIZE
    gate_rows = _gate_rows_per_group()
    w_g =jnp.pad(w_g.reshape(d, groups, per_group), ((0, 0), (0, 0), (0, gate_rows - per_group)))
    w_g_t = jnp.pad(w_g.reshape(d, groups * gate_rows).T, ((0, -groups * gate_rows % 128), (0, 0))).astype(BF16)

    kv = _kv_proj(h, w_kv, groups, dk)
    z = _matmul(h, w_z.astype(BF16))
    q_t = _proj_q_t(h, w_q_t, batch, seq, groups, dk, dk ** -0.5 * LOG2E)
    v_t = _proj_t(h, w_v_t, batch, seq, BF16, False, "nsa_in_v")
    gates_t = _proj_t(h, w_g_t, batch, seq, F32, True, "nsa_in_gates")
    kc, vct = _compress(kv, cmp_pos, ck_w1.astype(BF16), ck_w2.astype(BF16), cv_w1.astype(BF16),
                        cv_w2.T.astype(BF16), batch, seq)
    o = _attention(q_t, v_t, gates_t, kv, kc, vct, tables, batch, seq, heads, groups, dk, lay)
    return _nsa_out(o.reshape(batch * seq, q_w), z, x, w_out.astype(BF16), g_next, final)


def kernel(x, rel_bias, l0_norm, l0_w_in, l0_dw_w, l0_dw_b, l0_ln_g, l0_ln_b, l0_w_out, l1_norm, l1_w_in, l1_cmp_pos, l1_ck_w1, l1_ck_w2, l1_cv_w1, l1_cv_w2, l1_w_out, l2_norm, l2_w_in, l2_dw_w, l2_dw_b, l2_ln_g, l2_ln_b, l2_w_out, l3_norm, l3_w_in, l3_cmp_pos, l3_ck_w1, l3_ck_w2, l3_cv_w1, l3_cv_w2, l3_w_out, final_norm):
    batch, seq, d = x.shape
    heads = rel_bias.shape[1]
    lay = _table_layout(seq)
    tables = _bias_tables(rel_bias, seq, l1_cmp_pos.shape[0])
    x2 = x.reshape(batch * seq, d)
    h = _rmsnorm(x2, l0_norm)
    x2, h = _conformer_layer(x2, h, seq, l0_w_in, l0_dw_w, l0_dw_b, l0_ln_g, l0_ln_b, l0_w_out, l1_norm, False)
    x2, h = _nsa_layer(x2, h, batch, seq, heads, tables, lay, l1_w_in, l1_cmp_pos, l1_ck_w1, l1_ck_w2, l1_cv_w1,
                       l1_cv_w2, l1_w_out, l2_norm, False)
    x2, h = _conformer_layer(x2, h, seq, l2_w_in, l2_dw_w, l2_dw_b, l2_ln_g, l2_ln_b, l2_w_out, l3_norm, False)
    y = _nsa_layer(x2, h, batch, seq, heads, tables, lay, l3_w_in, l3_cmp_pos, l3_ck_w1, l3_ck_w2, l3_cv_w1,
                   l3_cv_w2, l3_w_out, final_norm, True)
    return y.reshape(batch, seq, d)
```

```python
import functools
import math

import jax
import jax.numpy as jnp
import numpy as np
from jax import lax
from jax.experimental import pallas as pl
from jax.experimental.pallas import tpu as pltpu

F32 = jnp.float32
BF16 = jnp.bfloat16

GROUP_SIZE = 4
CMP_STRIDE = 16
SEL_LEN = 64
N_SELECT = 8
WINDOW = 512
MAX_DISTANCE = 128
EPS = 1e-6
LOG2E = math.log2(math.e)
NEG = -1e30
FORCE = 1e6

TQ = 256
PROJ_ROWS = 1024
BRANCH_BUFFERS = 5
KEY_CHUNK = 64
V7X_VMEM_LIMIT_BYTES = 56 * 1024 * 1024
SUBLANES = 8
NORM_ROWS = 32


def _params(*semantics):
    return pltpu.CompilerParams(dimension_semantics=semantics, vmem_limit_bytes=V7X_VMEM_LIMIT_BYTES)


def _sigmoid(x):
    return 1.0 / (1.0 + jnp.exp(-x))


def _dot(a, b):
    return jnp.dot(a, b, preferred_element_type=F32)


def _dot_nt(a, b):
    return lax.dot_general(a, b, (((1,), (1,)), ((), ())), preferred_element_type=F32)


def _rmsnorm_body(x_ref, g_ref, o_ref):
    x = x_ref[...]
    ms = jnp.mean(x * x, axis=-1, keepdims=True)
    o_ref[...] = (x * lax.rsqrt(ms + EPS) * g_ref[...]).astype(o_ref.dtype)


def _rmsnorm(x2d, gain, tm=512):
    m, d = x2d.shape
    return pl.pallas_call(
        _rmsnorm_body,
        out_shape=jax.ShapeDtypeStruct((m, d), BF16),
        grid=(m // tm,),
        in_specs=[pl.BlockSpec((tm, d), lambda i: (i, 0)), pl.BlockSpec((1, d), lambda i: (0, 0))],
        out_specs=pl.BlockSpec((tm, d), lambda i: (i, 0)),
        compiler_params=_params("parallel"),
        name="rmsnorm",
    )(x2d, gain.reshape(1, d))


def _out_proj_tail(lhs_ref, w_ref, x_ref, gn_ref, xo_ref, ho_ref):
    tm = xo_ref.shape[0]
    xo_ref[...] = x_ref[...] + _dot(lhs_ref[...], w_ref[...])

    def chunk(c, carry):
        r0 = pl.multiple_of(c * NORM_ROWS, NORM_ROWS)
        xn = xo_ref[pl.ds(r0, NORM_ROWS), :]
        ms = jnp.mean(xn * xn, axis=-1, keepdims=True)
        hn = xn * lax.rsqrt(ms + EPS) * gn_ref[...]
        if ho_ref is None:
            xo_ref[pl.ds(r0, NORM_ROWS), :] = hn
        else:
            ho_ref[pl.ds(r0, NORM_ROWS), :] = hn.astype(ho_ref.dtype)
        return carry

    lax.fori_loop(0, tm // NORM_ROWS, chunk, 0)


def _out_proj_outputs(m, d, tm, final):
    x_spec = pl.BlockSpec((tm, d), lambda i: (i, 0))
    if final:
        return jax.ShapeDtypeStruct((m, d), F32), x_spec
    return ((jax.ShapeDtypeStruct((m, d), F32), jax.ShapeDtypeStruct((m, d), BF16)), (x_spec, x_spec))


def _conformer_in_body(h_ref, wa_ref, wb_ref, wz_ref, v_ref, sz_ref):
    h = h_ref[...]
    a = _dot(h, wa_ref[...])
    b = _dot(h, wb_ref[...])
    z = _dot(h, wz_ref[...])
    v_ref[...] = (a * _sigmoid(b)).astype(v_ref.dtype)
    sz_ref[...] = (z * _sigmoid(z)).astype(sz_ref.dtype)


def _conformer_in(h, w_in, tm=PROJ_ROWS, tn=512):
    m, d = h.shape
    e = w_in.shape[1] // 3
    nj = e // tn
    h_spec = pl.BlockSpec((tm, d), lambda i, j: (i, 0))
    o_spec = pl.BlockSpec((tm, tn), lambda i, j: (i, j))
    return pl.pallas_call(
        _conformer_in_body,
        out_shape=(jax.ShapeDtypeStruct((m, e), F32), jax.ShapeDtypeStruct((m, e), BF16)),
        grid=(m // tm, nj),
        in_specs=[h_spec,
                  pl.BlockSpec((d, tn), lambda i, j: (0, j)),
                  pl.BlockSpec((d, tn), lambda i, j: (0, j + nj)),
                  pl.BlockSpec((d, tn), lambda i, j: (0, j + 2 * nj))],
        out_specs=(o_spec, o_spec),
        compiler_params=_params("parallel", "parallel"),
        name="conformer_in",
    )(h, w_in, w_in, w_in)


CONV_HALO = 32
CONV_ROWS = 32
CONV_COLS = 512


def _conformer_out_body(tiles_per_seq, final, v_ref, halo_ref, sz_ref, x_ref, dww_ref, dwb_ref, lng_ref, lnb_ref,
                        w_ref, gn_ref, *rest):
    if final:
        xo_ref, vb_ref, sh_ref, y_ref, lhs_ref = rest
        ho_ref = None
    else:
        xo_ref, ho_ref, vb_ref, sh_ref, y_ref, lhs_ref = rest
    tm, e = v_ref.shape
    width = dww_ref.shape[0]
    first = (pl.program_id(0) % tiles_per_seq) == 0
    vb_ref[0:CONV_HALO, :] = jnp.where(first, 0.0, halo_ref[...])
    vb_ref[CONV_HALO:CONV_HALO + tm, :] = v_ref[...]
    vb_ref[CONV_HALO + tm:CONV_HALO + tm + SUBLANES, :] = jnp.zeros((SUBLANES, e), F32)

    for cb in range(e // CONV_COLS):
        cs = slice(cb * CONV_COLS, (cb + 1) * CONV_COLS)

        def shift_chunk(c, carry, cs=cs):
            r0 = pl.multiple_of(c * CONV_ROWS, CONV_ROWS)
            window = vb_ref[pl.ds(r0, CONV_ROWS + SUBLANES), cs]
            for r in range(1, SUBLANES):
                sh_ref[r - 1, pl.ds(r0, CONV_ROWS), :] = window[r:r + CONV_ROWS, :]
            return carry

        lax.fori_loop(0, (CONV_HALO + tm) // CONV_ROWS, shift_chunk, 0)

        def conv_chunk(c, carry, cs=cs):
            r0 = pl.multiple_of(c * CONV_ROWS, CONV_ROWS)
            acc = jnp.broadcast_to(dwb_ref[:, cs], (CONV_ROWS, CONV_COLS))
            for k in range(width):
                groups, r = divmod(CONV_HALO - (width - 1) + k, SUBLANES)
                rows = pl.ds(pl.multiple_of(r0 + groups * SUBLANES, SUBLANES), CONV_ROWS)
                src = vb_ref[rows, cs] if r == 0 else sh_ref[r - 1, rows, :]
                acc = acc + src * jnp.tile(dww_ref[k, :, cs], (CONV_ROWS // SUBLANES, 1))
            y_ref[pl.ds(r0, CONV_ROWS), cs] = acc
            return carry

        lax.fori_loop(0, tm // CONV_ROWS, conv_chunk, 0)

    def ln_chunk(c, carry):
        r0 = pl.multiple_of(c * NORM_ROWS, NORM_ROWS)
        y = y_ref[pl.ds(r0, NORM_ROWS), :]
        mu = jnp.mean(y, axis=-1, keepdims=True)
        dlt = y - mu
        var = jnp.mean(dlt * dlt, axis=-1, keepdims=True)
        yn = dlt * lax.rsqrt(var + EPS) * lng_ref[...] + lnb_ref[...]
        act = yn * _sigmoid(yn) * sz_ref[pl.ds(r0, NORM_ROWS), :].astype(F32)
        lhs_ref[pl.ds(r0, NORM_ROWS), :] = act.astype(lhs_ref.dtype)
        return carry

    lax.fori_loop(0, tm // NORM_ROWS, ln_chunk, 0)
    _out_proj_tail(lhs_ref, w_ref, x_ref, gn_ref, xo_ref, ho_ref)


def _conformer_out(v, sz, x, dw_w, dw_b, ln_g, ln_b, w_out, g_next, seq, final, tm=256):
    m, e = v.shape
    d = w_out.shape[1]
    width = dw_w.shape[0]
    assert width - 1 <= CONV_HALO and seq % tm == 0 and tm % CONV_HALO == 0
    halo_blocks = tm // CONV_HALO
    row = lambda a: a.reshape(1, -1)
    out_shape, out_specs = _out_proj_outputs(m, d, tm, final)
    return pl.pallas_call(
        functools.partial(_conformer_out_body, seq // tm, final),
        out_shape=out_shape,
        grid=(m // tm,),
        in_specs=[pl.BlockSpec((tm, e), lambda i: (i, 0)),
                  pl.BlockSpec((CONV_HALO, e), lambda i: (jnp.maximum(i * halo_blocks - 1, 0), 0)),
                  pl.BlockSpec((tm, e), lambda i: (i, 0)),
                  pl.BlockSpec((tm, d), lambda i: (i, 0)),
                  pl.BlockSpec((width, SUBLANES, e), lambda i: (0, 0, 0)),
                  pl.BlockSpec((1, e), lambda i: (0, 0)),
                  pl.BlockSpec((1, e), lambda i: (0, 0)),
                  pl.BlockSpec((1, e), lambda i: (0, 0)),
                  pl.BlockSpec((e, d), lambda i: (0, 0)),
                  pl.BlockSpec((1, d), lambda i: (0, 0))],
        out_specs=out_specs,
        scratch_shapes=[pltpu.VMEM((CONV_HALO + tm + SUBLANES, e), F32),
                        pltpu.VMEM((SUBLANES - 1, CONV_HALO + tm, CONV_COLS), F32),
                        pltpu.VMEM((tm, e), F32), pltpu.VMEM((tm, e), BF16)],
        compiler_params=_params("parallel"),
        name="conformer_out",
    )(v, v, sz, x, jnp.broadcast_to(dw_w[:, None, :], (width, SUBLANES, e)), row(dw_b), row(ln_g), row(ln_b), w_out, row(g_next))


def _matmul_body(h_ref, w_ref, o_ref):
    o_ref[...] = _dot(h_ref[...], w_ref[...]).astype(o_ref.dtype)


def _matmul(h, w, tm=PROJ_ROWS, tn=512):
    m, d = h.shape
    n = w.shape[1]
    return pl.pallas_call(
        _matmul_body,
        out_shape=jax.ShapeDtypeStruct((m, n), BF16),
        grid=(m // tm, n // tn),
        in_specs=[pl.BlockSpec((tm, d), lambda i, j: (i, 0)), pl.BlockSpec((d, tn), lambda i, j: (0, j))],
        out_specs=pl.BlockSpec((tm, tn), lambda i, j: (i, j)),
        compiler_params=_params("parallel", "parallel"),
        name="nsa_in_z",
    )(h, w)


def _kv_proj_body(h_ref, w_ref, o_ref):
    groups, _, dk = o_ref.shape[1:]
    res = _dot(h_ref[...], w_ref[...]).astype(o_ref.dtype)
    for g in range(groups):
        o_ref[0, g] = res[:, g * dk:(g + 1) * dk]


def _kv_proj(h, w, groups, dk, tm=PROJ_ROWS):
    m, d = h.shape
    kinds = w.shape[1] // (groups * dk)
    return pl.pallas_call(
        _kv_proj_body,
        out_shape=jax.ShapeDtypeStruct((kinds, groups, m, dk), BF16),
        grid=(m // tm, kinds),
        in_specs=[pl.BlockSpec((tm, d), lambda i, j: (i, 0)), pl.BlockSpec((d, groups * dk), lambda i, j: (0, j))],
        out_specs=pl.BlockSpec((1, groups, tm, dk), lambda i, j: (j, 0, i, 0)),
        compiler_params=_params("parallel", "parallel"),
        name="nsa_in_kv",
    )(h, w)


def _proj_q_t_body(scale, w_ref, h_ref, o_ref):
    dk = o_ref.shape[3]
    res = (_dot_nt(w_ref[...], h_ref[...]) * scale).astype(o_ref.dtype)
    for c in range(o_ref.shape[1]):
        for r in range(w_ref.shape[0] // dk):
            o_ref[0, c, 0, :, r * TQ:(r + 1) * TQ] = res[r * dk:(r + 1) * dk, c * TQ:(c + 1) * TQ]


def _proj_q_t(h, w_t, batch, seq, groups, dk, scale, tm=PROJ_ROWS):
    m, d = h.shape
    rows = w_t.shape[0] // groups
    tiles = seq // tm
    return pl.pallas_call(
        functools.partial(_proj_q_t_body, scale),
        out_shape=jax.ShapeDtypeStruct((batch, seq // TQ, groups, dk, rows // dk * TQ), BF16),
        grid=(m // tm, groups),
        in_specs=[pl.BlockSpec((rows, d), lambda i, g: (g, 0)), pl.BlockSpec((tm, d), lambda i, g: (i, 0))],
        out_specs=pl.BlockSpec((1, tm // TQ, 1, dk, rows // dk * TQ), lambda i, g: (i // tiles, i % tiles, g, 0, 0)),
        compiler_params=_params("parallel", "parallel"),
        name="nsa_in_q",
    )(w_t, h)


def _proj_t_body(sigmoid, w_ref, h_ref, o_ref):
    res = _dot_nt(w_ref[...], h_ref[...])
    if sigmoid:
        res = _sigmoid(res)
    res = res.astype(o_ref.dtype)
    for c in range(o_ref.shape[1]):
        o_ref[0, c] = res[:, c * TQ:(c + 1) * TQ]


def _proj_t(h, w_t, batch, seq, out_dtype, sigmoid, name, tm=PROJ_ROWS):
    m, d = h.shape
    rows = w_t.shape[0]
    tn = next(t for t in (512, 256, 128) if rows % t == 0)
    tiles = seq // tm
    return pl.pallas_call(
        functools.partial(_proj_t_body, sigmoid),
        out_shape=jax.ShapeDtypeStruct((batch, seq // TQ, rows, TQ), out_dtype),
        grid=(m // tm, rows // tn),
        in_specs=[pl.BlockSpec((tn, d), lambda i, j: (j, 0)), pl.BlockSpec((tm, d), lambda i, j: (i, 0))],
        out_specs=pl.BlockSpec((1, tm // TQ, tn, TQ), lambda i, j: (i // tiles, i % tiles, j, 0)),
        compiler_params=_params("parallel", "parallel"),
        name=name,
    )(w_t, h)


def _compress_body(xk_ref, xv_ref, pos_ref, w1k_ref, w2k_ref, w1v_ref, w2v_ref, kc_ref, vct_ref):
    slots = xk_ref.shape[2]

    def hidden(x_ref, w1_ref):
        x = x_ref[0, 0].astype(F32)
        first = _dot((x + pos_ref[0]).astype(BF16), w1_ref[0])
        second = _dot((x + pos_ref[1]).astype(BF16), w1_ref[1])
        pre = first + pltpu.roll(second, slots - 1, 0)
        return (pre * _sigmoid(pre)).astype(BF16)

    kc_ref[0, 0] = _dot(hidden(xk_ref, w1k_ref), w2k_ref[...]).astype(kc_ref.dtype)
    vct_ref[0, 0] = _dot_nt(w2v_ref[...], hidden(xv_ref, w1v_ref)).astype(vct_ref.dtype)


def _compress(kv, pos, w1k, w2k, w1v, w2v, batch, seq):
    kinds, groups, m, dk = kv.shape
    slots = seq // CMP_STRIDE
    assert pos.shape[0] == 2 * CMP_STRIDE
    x = kv[:2].reshape(2, groups, m // CMP_STRIDE, CMP_STRIDE * dk)
    pos8 = pos.reshape(2, 1, CMP_STRIDE * dk)
    w1k = w1k.reshape(2, CMP_STRIDE * dk, dk)
    w1v = w1v.reshape(2, CMP_STRIDE * dk, dk)
    full = lambda a: pl.BlockSpec(a.shape, lambda b, g: (0,) * a.ndim)
    return pl.pallas_call(
        _compress_body,
        out_shape=(jax.ShapeDtypeStruct((batch, groups, slots, dk), BF16),
                   jax.ShapeDtypeStruct((batch, groups, dk, slots), BF16)),
        grid=(batch, groups),
        in_specs=[pl.BlockSpec((1, 1, slots, CMP_STRIDE * dk), lambda b, g: (0, g, b, 0)),
                  pl.BlockSpec((1, 1, slots, CMP_STRIDE * dk), lambda b, g: (1, g, b, 0)),
                  full(pos8), full(w1k), full(w2k), full(w1v), full(w2v)],
        out_specs=(pl.BlockSpec((1, 1, slots, dk), lambda b, g: (b, g, 0, 0)),
                   pl.BlockSpec((1, 1, dk, slots), lambda b, g: (b, g, 0, 0))),
        compiler_params=_params("parallel", "parallel"),
        name="nsa_compress",
    )(x, x, pos8, w1k, w2k, w1v, w2v)


def _t5_bucket_np(dist, num_buckets):
    dist = np.maximum(dist, 0)
    max_exact = num_buckets // 2
    d = np.maximum(dist, max_exact).astype(np.float32)
    ratio = np.log(d / np.float32(max_exact)) / np.float32(math.log(MAX_DISTANCE / max_exact))
    large = max_exact + (ratio * np.float32(num_buckets - max_exact)).astype(np.int32)
    large = np.minimum(large, num_buckets - 1)
    return np.where(dist < max_exact, dist, large).astype(np.int32)


def _table_layout(seq):
    slots = seq // CMP_STRIDE
    cmp_rows = slots + (seq // TQ - 1) * (TQ // CMP_STRIDE)
    cmp_rows = -(-cmp_rows // 64) * 64
    return dict(win_far=0, far=TQ, prev=2 * TQ, diag=3 * TQ, cmp=4 * TQ, rows=4 * TQ + cmp_rows, cmp_rows=cmp_rows)


def _bucket_index_tables(seq, num_buckets, cmp_len):
    lay = _table_layout(seq)
    j = np.arange(TQ)[:, None]
    i = np.arange(TQ)[None, :]
    far_bucket = _t5_bucket_np(np.full((TQ, TQ), 2 * TQ), num_buckets)
    assert TQ >= MAX_DISTANCE
    win_far = np.where(i < j, far_bucket, -1)
    far = far_bucket
    prev = _t5_bucket_np(i - j + TQ, num_buckets)
    diag = np.where(i >= j, _t5_bucket_np(i - j, num_buckets), -1)
    rho = np.arange(lay["cmp_rows"])[:, None]
    rel_slot = rho - (seq // TQ - 1) * (TQ // CMP_STRIDE)
    dist_c = i - CMP_STRIDE * rel_slot - (cmp_len - 1)
    cmp = np.where(dist_c >= 0, _t5_bucket_np(dist_c, num_buckets), -1)
    return np.concatenate([win_far, far, prev, diag, cmp], axis=0).astype(np.int32)


TABLE_CHUNK = 64


def _bias_table_body(rb_ref, idx_ref, o_ref):
    head = pl.program_id(0)
    num_buckets = rb_ref.shape[0]

    def chunk(c, carry):
        r0 = pl.multiple_of(c * TABLE_CHUNK, TABLE_CHUNK)
        idx = idx_ref[pl.ds(r0, TABLE_CHUNK), :]
        acc = jnp.full(idx.shape, NEG, F32)
        for k in range(num_buckets):
            acc = jnp.where(idx == k, rb_ref[k, head] * LOG2E, acc)
        o_ref[0, pl.ds(r0, TABLE_CHUNK), :] = acc
        return carry

    lax.fori_loop(0, idx_ref.shape[0] // TABLE_CHUNK, chunk, 0)


def _bias_tables(rel_bias, seq, cmp_len):
    num_buckets, heads = rel_bias.shape
    idx = jnp.asarray(_bucket_index_tables(seq, num_buckets, cmp_len))
    rows = idx.shape[0]
    return pl.pallas_call(
        _bias_table_body,
        out_shape=jax.ShapeDtypeStruct((heads, rows, TQ), F32),
        grid=(heads,),
        in_specs=[pl.BlockSpec(memory_space=pltpu.SMEM), pl.BlockSpec((rows, TQ), lambda h: (0, 0))],
        out_specs=pl.BlockSpec((1, rows, TQ), lambda h: (h, 0, 0)),
        compiler_params=_params("parallel"),
        name="t5_bias_tables",
    )(rel_bias, idx)


def _attention_body(seq, lay, qt_ref, kc_ref, vct_ref, ks_ref, kw_ref, vst_ref, vwt_ref, gt_ref, tab_ref, ovl_ref,
                    o_ref, oc_ref, accs_ref, accw_ref, selb_ref, *buffers):
    qb = pl.program_id(2)
    heads_per_group = tab_ref.shape[0]
    dk = kc_ref.shape[3]
    slots = kc_ref.shape[2]
    n_blocks = ovl_ref.shape[0]
    n_tiles = seq // TQ
    slots_per_tile = TQ // CMP_STRIDE
    blocks_per_tile = TQ // SEL_LEN
    sel_shift = SEL_LEN.bit_length() - 1

    heads = range(heads_per_group)
    lanes = [slice(r * TQ, (r + 1) * TQ) for r in heads]
    q_all = qt_ref[0, 0, 0]

    def key_rows(ref, kt):
        return ref[0, 0, pl.ds(pl.multiple_of(kt * TQ, TQ), TQ), :]

    def branch_buffers(which):
        base = BRANCH_BUFFERS * which
        return buffers[base:base + 2], buffers[base + 2:base + 4], buffers[base + 4]

    for which, acc_ref, first_keys in ((0, accs_ref, key_rows(ks_ref, 0)), (1, accw_ref, key_rows(kw_ref, qb))):
        s_refs, p_refs, _ = branch_buffers(which)
        acc_ref[...] = jnp.zeros(acc_ref.shape, F32)
        p_refs[1][...] = jnp.zeros(p_refs[1].shape, BF16)
        for r in heads:
            s_refs[0][r] = _dot(first_keys, q_all[:, lanes[r]])

    cmp_row0 = lay["cmp"] + pl.multiple_of((n_tiles - 1 - qb) * slots_per_tile, slots_per_tile)
    s_cmp = _dot(kc_ref[0, 0], q_all)
    p_sum = jnp.zeros((slots, TQ), F32)
    p_cmp = []
    for r in heads:
        s = s_cmp[:, lanes[r]] + tab_ref[r, pl.ds(cmp_row0, slots), :]
        valid = s > 0.5 * NEG
        m = jnp.max(s, axis=0, keepdims=True)
        p = jnp.where(valid, jnp.exp2(s - m), 0.0)
        l = jnp.sum(p, axis=0, keepdims=True)
        p = p * jnp.where(l > 0.0, 1.0 / l, 0.0)
        p_sum = p_sum + p
        p_cmp.append(p.astype(BF16))
    oc_ref[...] = _dot(vct_ref[0, 0], jnp.concatenate(p_cmp, axis=1))

    p_hi = p_sum.astype(BF16)
    p_lo = (p_sum - p_hi.astype(F32)).astype(BF16)
    ovl = ovl_ref[...]
    imp = _dot(ovl, p_hi) + _dot(ovl, p_lo)
    blk = lax.broadcasted_iota(jnp.int32, (n_blocks, TQ), 0)
    tq = qb * TQ + lax.broadcasted_iota(jnp.int32, (n_blocks, TQ), 1)
    cur = tq >> sel_shift
    forced = (blk == 0) | (blk == cur) | (blk == cur - 1)
    imp = jnp.where(forced, FORCE, imp)
    imp = jnp.where((blk << sel_shift) <= tq, imp, NEG)
    selb = jnp.full((n_blocks, TQ), NEG, F32)
    for _ in range(min(N_SELECT, n_blocks)):
        top = jnp.max(imp, axis=0, keepdims=True)
        first = jnp.min(jnp.where(imp == top, blk, n_blocks), axis=0, keepdims=True)
        pick = blk == first
        selb = jnp.where(pick, 0.0, selb)
        imp = jnp.where(pick, -jnp.inf, imp)
    for b in range(n_blocks):
        selb_ref[b] = jnp.broadcast_to(selb[b:b + 1, :], (SUBLANES, TQ))

    chunks = [slice(c, c + KEY_CHUNK) for c in range(0, TQ, KEY_CHUNK)]

    def sel_bias(kt):
        tile = jnp.clip(kt - qb + 2, 0, 2)
        row0 = pl.multiple_of(lay["far"] + tile * TQ, TQ)
        masks = [jnp.tile(selb_ref[kt * blocks_per_tile + c.start // SEL_LEN], (KEY_CHUNK // SUBLANES, 1))
                 for c in chunks]
        return lambda r, n: tab_ref[r, pl.ds(row0 + chunks[n].start, KEY_CHUNK), :] + masks[n]

    def win_bias(it):
        row0 = pl.multiple_of(jnp.where(it == 0, lay["diag"], jnp.where(it == 1, lay["prev"], lay["win_far"])), TQ)
        return lambda r, n: tab_ref[r, pl.ds(row0 + chunks[n].start, KEY_CHUNK), :]

    def branch(which, n_steps, keys_of, value_of, bias_of, acc_ref):
        last = n_steps - 1
        s_refs, p_refs, p_last_ref = branch_buffers(which)

        def scores(i, s_ref):
            k = keys_of(jnp.minimum(i, last))
            for r in heads:
                s_ref[r] = _dot(k, q_all[:, lanes[r]])

        def accumulate(i, alphas, p_ref):
            v_t = value_of(jnp.maximum(i, 0))
            for r in heads:
                acc_ref[r] = alphas[r] * acc_ref[r] + _dot(v_t, p_ref[r])

        def half_step(parity, i, carry):
            s_ref, p_ref = s_refs[parity], p_refs[parity]
            ms, ls, alphas = carry
            accumulate(i - 1, alphas, p_refs[1 - parity])
            bias = bias_of(i)
            new_m, new_l, new_a = [], [], []
            for r in heads:
                m = ms[r]
                for n, c in enumerate(chunks):
                    s = s_ref[r, c, :] + bias(r, n)
                    s_ref[r, c, :] = s
                    m = jnp.maximum(m, jnp.max(s, axis=0, keepdims=True))
                alpha = jnp.exp2(ms[r] - m)
                l = alpha * ls[r]
                for c in chunks:
                    p = jnp.exp2(s_ref[r, c, :] - m)
                    l = l + jnp.sum(p, axis=0, keepdims=True)
                    p_ref[r, c, :] = p.astype(BF16)
                    p_last_ref[r, c, :] = p.astype(BF16)
                new_m.append(m)
                new_l.append(l)
                new_a.append(alpha)
            scores(i + 1, s_refs[1 - parity])
            return tuple(new_m), tuple(new_l), tuple(new_a)

        def step(i, carry):
            return lax.cond(i % 2 == 0, functools.partial(half_step, 0, i), functools.partial(half_step, 1, i), carry)

        init = (tuple(jnp.full((1, TQ), NEG, F32) for _ in heads), tuple(jnp.zeros((1, TQ), F32) for _ in heads),
                tuple(jnp.ones((1, TQ), F32) for _ in heads))
        _, ls, alphas = lax.fori_loop(0, n_steps, step, init)
        return ls, lambda: accumulate(last, alphas, p_last_ref)

    l_s, finish_s = branch(0, qb + 1, lambda kt: key_rows(ks_ref, kt), lambda kt: vst_ref[0, kt], sel_bias, accs_ref)
    l_w, finish_w = branch(1, jnp.minimum(qb, WINDOW // TQ) + 1, lambda it: key_rows(kw_ref, qb - it),
                           lambda it: vwt_ref[0, qb - it], win_bias, accw_ref)
    finish_s()
    finish_w()

    gates = gt_ref[0, 0]
    for r in heads:
        g_c = gates[3 * r:3 * r + 1, :]
        g_s = gates[3 * r + 1:3 * r + 2, :]
        g_w = gates[3 * r + 2:3 * r + 3, :]
        out_t = g_c * oc_ref[:, lanes[r]] + (g_s / l_s[r]) * accs_ref[r] + (g_w / l_w[r]) * accw_ref[r]
        o_ref[0, :, r * dk:(r + 1) * dk] = out_t.T.astype(o_ref.dtype)


def _gate_rows_per_group():
    return -(-3 * GROUP_SIZE // SUBLANES) * SUBLANES


def _attention(q_t, v_t, gates_t, kv, kc, vct, tables, batch, seq, heads, groups, dk, lay):
    assert WINDOW == 2 * TQ and SEL_LEN % KEY_CHUNK == 0 and seq % TQ == 0
    r = heads // groups
    n_tiles = seq // TQ
    slots = seq // CMP_STRIDE
    n_blocks = seq // SEL_LEN
    gate_rows = _gate_rows_per_group()
    n0 = np.arange(slots)[None, :] * CMP_STRIDE
    s0 = np.arange(n_blocks)[:, None] * SEL_LEN
    ovl = jnp.asarray(((n0 < s0 + SEL_LEN) & (n0 + 2 * CMP_STRIDE > s0) & (n0 + 2 * CMP_STRIDE <= seq)), dtype=BF16)
    return pl.pallas_call(
        functools.partial(_attention_body, seq, lay),
        out_shape=jax.ShapeDtypeStruct((batch, seq, heads * dk), BF16),
        grid=(batch, groups, n_tiles),
        in_specs=[
            pl.BlockSpec((1, 1, 1, dk, r * TQ), lambda b, g, t: (b, t, g, 0, 0)),
            pl.BlockSpec((1, 1, slots, dk), lambda b, g, t: (b, g, 0, 0)),
            pl.BlockSpec((1, 1, dk, slots), lambda b, g, t: (b, g, 0, 0)),
            pl.BlockSpec((1, 1, seq, dk), lambda b, g, t: (2, g, b, 0)),
            pl.BlockSpec((1, 1, seq, dk), lambda b, g, t: (3, g, b, 0)),
            pl.BlockSpec((1, n_tiles, dk, TQ), lambda b, g, t: (b, 0, g, 0)),
            pl.BlockSpec((1, n_tiles, dk, TQ), lambda b, g, t: (b, 0, groups + g, 0)),
            pl.BlockSpec((1, 1, gate_rows, TQ), lambda b, g, t: (b, t, g, 0)),
            pl.BlockSpec((r, lay["rows"], TQ), lambda b, g, t: (g, 0, 0)),
            pl.BlockSpec((n_blocks, slots), lambda b, g, t: (0, 0)),
        ],
        out_specs=pl.BlockSpec((1, TQ, r * dk), lambda b, g, t: (b, t, g)),
        scratch_shapes=[pltpu.VMEM((dk, r * TQ), F32), pltpu.VMEM((r, dk, TQ), F32), pltpu.VMEM((r, dk, TQ), F32),
                        pltpu.VMEM((n_blocks, SUBLANES, TQ), F32),
                        *([pltpu.VMEM((r, TQ, TQ), F32)] * 2 + [pltpu.VMEM((r, TQ, TQ), BF16)] * 3) * 2],
        compiler_params=_params("parallel", "parallel", "arbitrary"),
        name="nsa_attention",
    )(q_t, kc, vct, kv, kv, v_t, v_t, gates_t, tables, ovl)


def _nsa_out_body(final, o_ref, z_ref, x_ref, w_ref, gn_ref, *rest):
    if final:
        xo_ref, lhs_ref = rest
        ho_ref = None
    else:
        xo_ref, ho_ref, lhs_ref = rest
    tm = o_ref.shape[0]

    def chunk(c, carry):
        r0 = pl.multiple_of(c * NORM_ROWS, NORM_ROWS)
        z = z_ref[pl.ds(r0, NORM_ROWS), :].astype(F32)
        o = o_ref[pl.ds(r0, NORM_ROWS), :].astype(F32)
        lhs_ref[pl.ds(r0, NORM_ROWS), :] = (o * (z * _sigmoid(z))).astype(lhs_ref.dtype)
        return carry

    lax.fori_loop(0, tm // NORM_ROWS, chunk, 0)
    _out_proj_tail(lhs_ref, w_ref, x_ref, gn_ref, xo_ref, ho_ref)


def _nsa_out(o, z, x, w_out, g_next, final, tm=256):
    m, e = o.shape
    d = w_out.shape[1]
    out_shape, out_specs = _out_proj_outputs(m, d, tm, final)
    return pl.pallas_call(
        functools.partial(_nsa_out_body, final),
        out_shape=out_shape,
        grid=(m // tm,),
        in_specs=[pl.BlockSpec((tm, e), lambda i: (i, 0)),
                  pl.BlockSpec((tm, e), lambda i: (i, 0)),
                  pl.BlockSpec((tm, d), lambda i: (i, 0)),
                  pl.BlockSpec((e, d), lambda i: (0, 0)),
                  pl.BlockSpec((1, d), lambda i: (0, 0))],
        out_specs=out_specs,
        scratch_shapes=[pltpu.VMEM((tm, e), BF16)],
        compiler_params=_params("parallel"),
        name="nsa_out",
    )(o, z, x, w_out, g_next.reshape(1, d))


def _conformer_layer(x, h, seq, w_in, dw_w, dw_b, ln_g, ln_b, w_out, g_next, final):
    v, sz = _conformer_in(h, w_in.astype(BF16))
    return _conformer_out(v, sz, x, dw_w, dw_b, ln_g, ln_b, w_out.astype(BF16), g_next, seq, final)


def _nsa_layer(x, h, batch, seq, heads, tables, lay, w_in, cmp_pos, ck_w1, ck_w2, cv_w1, cv_w2, w_out, g_next, final):
    d = x.shape[1]
    dk = d // heads
    groups = heads // GROUP_SIZE
    q_w = heads * dk
    kv_w = groups * dk
    gate_w = 3 * heads
    assert w_in.shape[1] == 2 * q_w + 6 * kv_w + gate_w
    cuts = np.cumsum([0, q_w] + [kv_w] * 6 + [gate_w, q_w])
    col = lambda k: w_in[:, cuts[k]:cuts[k + 1]]
    w_q, w_kc, w_vc, w_ks, w_vs, w_kw, w_vw, w_g, w_z = (col(k) for k in range(9))
    w_kv = jnp.concatenate([w_kc, w_vc, w_ks, w_kw], axis=1).astype(BF16)
    w_q_t = w_q.T.astype(BF16)
    w_v_t = jnp.concatenate([w_vs, w_vw], axis=1).T.astype(BF16)
    per_group = 3 * GROUP_SIZE
    gate_rows = _gate_rows_per_group()
    w_g = jnp.pad(w_g.reshape(d, groups, per_group), ((0, 0), (0, 0), (0, gate_rows - per_group)))
    w_g_t = jnp.pad(w_g.reshape(d, groups * gate_rows).T, ((0, -groups * gate_rows % 128), (0, 0))).astype(BF16)

    kv = _kv_proj(h, w_kv, groups, dk)
    z = _matmul(h, w_z.astype(BF16))
    q_t = _proj_q_t(h, w_q_t, batch, seq, groups, dk, dk ** -0.5 * LOG2E)
    v_t = _proj_t(h, w_v_t, batch, seq, BF16, False, "nsa_in_v")
    gates_t = _proj_t(h, w_g_t, batch, seq, F32, True, "nsa_in_gates")
    kc, vct = _compress(kv, cmp_pos, ck_w1.astype(BF16), ck_w2.astype(BF16), cv_w1.astype(BF16),
                        cv_w2.T.astype(BF16), batch, seq)
    o = _attention(q_t, v_t, gates_t, kv, kc, vct, tables, batch, seq, heads, groups, dk, lay)
    return _nsa_out(o.reshape(batch * seq, q_w), z, x, w_out.astype(BF16), g_next, final)


def kernel(x, rel_bias, l0_norm, l0_w_in, l0_dw_w, l0_dw_b, l0_ln_g, l0_ln_b, l0_w_out, l1_norm, l1_w_in, l1_cmp_pos, l1_ck_w1, l1_ck_w2, l1_cv_w1, l1_cv_w2, l1_w_out, l2_norm, l2_w_in, l2_dw_w, l2_dw_b, l2_ln_g, l2_ln_b, l2_w_out, l3_norm, l3_w_in, l3_cmp_pos, l3_ck_w1, l3_ck_w2, l3_cv_w1, l3_cv_w2, l3_w_out, final_norm):
    batch, seq, d = x.shape
    heads = rel_bias.shape[1]
    lay = _table_layout(seq)
    tables = _bias_tables(rel_bias, seq, l1_cmp_pos.shape[0])
    x2 = x.reshape(batch * seq, d)
    h = _rmsnorm(x2, l0_norm)
    x2, h = _conformer_layer(x2, h, seq, l0_w_in, l0_dw_w, l0_dw_b, l0_ln_g, l0_ln_b, l0_w_out, l1_norm, False)
    x2, h = _nsa_layer(x2, h, batch, seq, heads, tables, lay, l1_w_in, l1_cmp_pos, l1_ck_w1, l1_ck_w2, l1_cv_w1,
                       l1_cv_w2, l1_w_out, l2_norm, False)
    x2, h = _conformer_layer(x2, h, seq, l2_w_in, l2_dw_w, l2_dw_b, l2_ln_g, l2_ln_b, l2_w_out, l3_norm, False)
    y = _nsa_layer(x2, h, batch, seq, heads, tables, lay, l3_w_in, l3_cmp_pos, l3_ck_w1, l3_ck_w2, l3_cv_w1,
                   l3_cv_w2, l3_w_out, final_norm, True)
    return y.reshape(batch, seq, d)
```
